```python
import jax, jax.numpy as jnp
from jax import lax
import numpy as np

D_MODEL = 1024
BATCH = 4
SEQ = 8192
DEPTH = 1

HEAD_DIM = 64
N_HEADS = D_MODEL // HEAD_DIM
N_FOX_HEADS = N_HEADS // 2
N_SB_HEADS = N_HEADS - N_FOX_HEADS
FOX_WIDTH = N_FOX_HEADS * HEAD_DIM
SB_WIDTH = N_SB_HEADS * HEAD_DIM
MIX_WIDTH = FOX_WIDTH + SB_WIDTH
IN_COLS = 3 * FOX_WIDTH + N_FOX_HEADS + 3 * SB_WIDTH
D_FF = ((8 * D_MODEL // 3 + 127) // 128) * 128
CONV_WIDTH = 3
BLOCK_Q = 128
EPS = 1e-6

kernel_name = "hybrid_fox_stickbreaking_convffn"


def rmsnorm(x, g):
    xf = x.astype(jnp.float32)
    y = xf * lax.rsqrt(jnp.mean(xf * xf, axis=-1, keepdims=True) + EPS)
    return (y * g.astype(jnp.float32)).astype(x.dtype)


def split_heads(t, n_heads):
    b, s, _ = t.shape
    return t.reshape(b, s, n_heads, HEAD_DIM).transpose(0, 2, 1, 3)


def merge_heads(t):
    b, h, s, d = t.shape
    return t.transpose(0, 2, 1, 3).reshape(b, s, h * d)


def to_blocks(t):
    b, h, s = t.shape[:3]
    t = t.reshape(b, h, s // BLOCK_Q, BLOCK_Q, *t.shape[3:])
    return jnp.moveaxis(t, 2, 0)


def from_blocks(t):
    nb, b, h, bq, d = t.shape
    return jnp.moveaxis(t, 0, 2).reshape(b, h, nb * bq, d)


def fox_attention(q, k, v, log_f):
    s_len = q.shape[2]
    scale = HEAD_DIM ** -0.5
    kf = k.astype(jnp.float32)
    vf = v.astype(jnp.float32)
    F = jnp.cumsum(log_f, axis=-1)
    kpos = jnp.arange(s_len)

    def block(args):
        i, qi, Fi = args
        qpos = i * BLOCK_Q + jnp.arange(BLOCK_Q)
        logits = (jnp.einsum('bhqd,bhkd->bhqk', qi.astype(jnp.float32), kf) * scale
                  + Fi[..., :, None] - F[:, :, None, :])
        logits = jnp.where(kpos[None, :] <= qpos[:, None], logits, -jnp.inf)
        p = jax.nn.softmax(logits, axis=-1)
        return jnp.einsum('bhqk,bhkd->bhqd', p, vf)

    out = lax.map(block, (jnp.arange(s_len // BLOCK_Q), to_blocks(q), to_blocks(F)))
    return from_blocks(out).astype(v.dtype)


def stick_breaking_attention(q, k, v):
    s_len = q.shape[2]
    scale = HEAD_DIM ** -0.5
    kf = k.astype(jnp.float32)
    vf = v.astype(jnp.float32)
    kpos = jnp.arange(s_len)

    def block(args):
        i, qi = args
        qpos = i * BLOCK_Q + jnp.arange(BLOCK_Q)
        mask = kpos[None, :] < qpos[:, None]
        z = jnp.einsum('bhqd,bhkd->bhqk', qi.astype(jnp.float32), kf) * scale
        log_beta = jax.nn.log_sigmoid(z)
        log_one_minus = jnp.where(mask, jax.nn.log_sigmoid(-z), 0.0)
        after = lax.cumsum(log_one_minus, axis=3, reverse=True) - log_one_minus
        weights = jnp.where(mask, jnp.exp(log_beta + after), 0.0)
        return jnp.einsum('bhqk,bhkd->bhqd', weights, vf)

    out = lax.map(block, (jnp.arange(s_len // BLOCK_Q), to_blocks(q)))
    return from_blocks(out).astype(v.dtype)


def causal_dwconv(u, w, b):
    s_len = u.shape[1]
    up = jnp.pad(u, ((0, 0), (CONV_WIDTH - 1, 0), (0, 0)))
    out = b + w[0] * up[:, 0:s_len]
    for kk in range(1, CONV_WIDTH):
        out = out + w[kk] * up[:, kk:kk + s_len]
    return out


def setup_inputs(seed: int = 0) -> dict:
    key = jax.random.key(seed)
    ks = jax.random.split(key, 13)
    f32 = jnp.float32
    x = jax.random.normal(ks[0], (BATCH, SEQ, D_MODEL), f32)
    attn_norm_g = 1.0 + 0.05 * jax.random.normal(ks[1], (DEPTH, D_MODEL), f32)
    w_in = jax.random.normal(ks[2], (DEPTH, D_MODEL, IN_COLS), f32) * D_MODEL ** -0.5
    forget_bias = (jnp.linspace(1.0, 6.0, N_FOX_HEADS, dtype=f32)[None, :]
                   + 0.1 * jax.random.normal(ks[3], (DEPTH, N_FOX_HEADS), f32))
    fox_out_g = 1.0 + 0.05 * jax.random.normal(ks[4], (DEPTH, FOX_WIDTH), f32)
    sb_out_g = 1.0 + 0.05 * jax.random.normal(ks[5], (DEPTH, SB_WIDTH), f32)
    w_out = jax.random.normal(ks[6], (DEPTH, MIX_WIDTH, D_MODEL), f32) * MIX_WIDTH ** -0.5
    ffn_norm_g = 1.0 + 0.05 * jax.random.normal(ks[7], (DEPTH, D_MODEL), f32)
    w_up = jax.random.normal(ks[8], (DEPTH, D_MODEL, 2 * D_FF), f32) * D_MODEL ** -0.5
    conv_w = jax.random.normal(ks[9], (DEPTH, CONV_WIDTH, 2 * D_FF), f32) * CONV_WIDTH ** -0.5
    conv_b = 0.02 * jax.random.normal(ks[10], (DEPTH, 2 * D_FF), f32)
    w_down = jax.random.normal(ks[11], (DEPTH, D_FF, D_MODEL), f32) * D_FF ** -0.5
    final_norm_g = 1.0 + 0.05 * jax.random.normal(ks[12], (D_MODEL,), f32)
    return {"x": x, "attn_norm_g": attn_norm_g, "w_in": w_in, "forget_bias": forget_bias,
            "fox_out_g": fox_out_g, "sb_out_g": sb_out_g, "w_out": w_out,
            "ffn_norm_g": ffn_norm_g, "w_up": w_up, "conv_w": conv_w, "conv_b": conv_b,
            "w_down": w_down, "final_norm_g": final_norm_g}


def reference(x, attn_norm_g, w_in, forget_bias, fox_out_g, sb_out_g, w_out,
              ffn_norm_g, w_up, conv_w, conv_b, w_down, final_norm_g):
    b, s_len, _ = x.shape
    splits = [FOX_WIDTH, 2 * FOX_WIDTH, 3 * FOX_WIDTH, 3 * FOX_WIDTH + N_FOX_HEADS,
              3 * FOX_WIDTH + N_FOX_HEADS + SB_WIDTH,
              3 * FOX_WIDTH + N_FOX_HEADS + 2 * SB_WIDTH]
    for l in range(DEPTH):
        h = rmsnorm(x, attn_norm_g[l])
        proj = h @ w_in[l]
        fq, fk, fv, f_logit, sq, sk, sv = jnp.split(proj, splits, axis=-1)
        log_f = jax.nn.log_sigmoid(
            (f_logit + forget_bias[l]).astype(jnp.float32)).transpose(0, 2, 1)
        o_fox = fox_attention(split_heads(fq, N_FOX_HEADS), split_heads(fk, N_FOX_HEADS),
                              split_heads(fv, N_FOX_HEADS), log_f)
        o_sb = stick_breaking_attention(split_heads(sq, N_SB_HEADS), split_heads(sk, N_SB_HEADS),
                                        split_heads(sv, N_SB_HEADS))
        o = jnp.concatenate([rmsnorm(merge_heads(o_fox), fox_out_g[l]),
                             rmsnorm(merge_heads(o_sb), sb_out_g[l])], axis=-1)
        x = x + o @ w_out[l]
        h = rmsnorm(x, ffn_norm_g[l])
        u = causal_dwconv(h @ w_up[l], conv_w[l], conv_b[l])
        gate, val = jnp.split(u, 2, axis=-1)
        x = x + (jax.nn.silu(gate) * val) @ w_down[l]
    return rmsnorm(x, final_norm_g)
```

```python
import functools

import jax
import jax.numpy as jnp
from jax import lax
from jax.experimental import pallas as pl
from jax.experimental.pallas import tpu as pltpu

HEAD_DIM = 64
EPS = 1e-6
CONV_WIDTH = 3

LANES = 128
HEADS_PER_STEP = LANES // HEAD_DIM
BF16_SUBLANES = 16
VMEM_LIMIT_BYTES = 56 * 1024 * 1024

ROW_TILE = 512
PROJ_COL_CHUNK = 512
Q_TILE = 256
K_TILE = 256
CUMSUM_CHUNK = 256
FFN_CHUNK = 256

F32 = jnp.float32
BF16 = jnp.bfloat16
_NT = (((1,), (1,)), ((), ()))


def _compiler_params(semantics):
    return pltpu.CompilerParams(dimension_semantics=semantics,
                                vmem_limit_bytes=VMEM_LIMIT_BYTES)


def _rmsnorm_bf16(x, g):
    ms = jnp.mean(x * x, axis=-1, keepdims=True)
    return (x * lax.rsqrt(ms + EPS) * g).astype(BF16)


def _log_sigmoid(x):
    return jnp.minimum(x, 0.0) - jnp.log1p(jnp.exp(-jnp.abs(x)))


def _in_proj_kernel(x_ref, g_ref, w_ref, wft_ref, qkv_ref, ft_ref, *, q_chunks, scale):
    h = _rmsnorm_bf16(x_ref[...], g_ref[...])
    n_cols = w_ref.shape[1]
    for c in range(n_cols // PROJ_COL_CHUNK):
        sl = slice(c * PROJ_COL_CHUNK, (c + 1) * PROJ_COL_CHUNK)
        y = jnp.dot(h, w_ref[:, sl], preferred_element_type=F32)
        if c in q_chunks:
            y = y * scale
        qkv_ref[:, sl] = y.astype(BF16)
    ft_ref[0] = lax.dot_general(wft_ref[...], h, _NT, preferred_element_type=F32)


def _in_proj(x2d, g, w, wft, *, batch, seq, q_chunks, scale):
    n_rows, d = x2d.shape
    n_cols = w.shape[1]
    n_heads = wft.shape[0]
    tiles_per_seq = seq // ROW_TILE
    return pl.pallas_call(
        functools.partial(_in_proj_kernel, q_chunks=q_chunks, scale=scale),
        grid=(n_rows // ROW_TILE,),
        in_specs=[
            pl.BlockSpec((ROW_TILE, d), lambda i: (i, 0)),
            pl.BlockSpec((1, d), lambda i: (0, 0)),
            pl.BlockSpec((d, n_cols), lambda i: (0, 0)),
            pl.BlockSpec((n_heads, d), lambda i: (0, 0)),
        ],
        out_specs=[
            pl.BlockSpec((ROW_TILE, n_cols), lambda i: (i, 0)),
            pl.BlockSpec((1, n_heads, ROW_TILE),
                         lambda i: (i // tiles_per_seq, 0, i % tiles_per_seq)),
        ],
        out_shape=[
            jax.ShapeDtypeStruct((n_rows, n_cols), BF16),
            jax.ShapeDtypeStruct((batch, n_heads, seq), F32),
        ],
        compiler_params=_compiler_params(("parallel",)),
        name="in_proj",
    )(x2d, g, w, wft)


def _forget_cumsum_kernel(ft_ref, b_ref, f_ref):
    rows, seq = ft_ref.shape
    r = lax.broadcasted_iota(jnp.int32, (CUMSUM_CHUNK, CUMSUM_CHUNK), 0)
    c = lax.broadcasted_iota(jnp.int32, (CUMSUM_CHUNK, CUMSUM_CHUNK), 1)
    upper = (r <= c).astype(F32)
    carry = jnp.zeros((rows, 1), F32)
    for ci in range(seq // CUMSUM_CHUNK):
        sl = slice(ci * CUMSUM_CHUNK, (ci + 1) * CUMSUM_CHUNK)
        log_f = _log_sigmoid(ft_ref[:, sl] + b_ref[...])
        local = jnp.dot(log_f, upper, precision=lax.Precision.HIGHEST,
                        preferred_element_type=F32)
        f_ref[:, sl] = local + carry
        carry = carry + local[:, CUMSUM_CHUNK - 1:CUMSUM_CHUNK]


def _forget_cumsum(ft2d, bias_col):
    rows, seq = ft2d.shape
    return pl.pallas_call(
        _forget_cumsum_kernel,
        grid=(1,),
        in_specs=[pl.BlockSpec((rows, seq), lambda i: (0, 0)),
                  pl.BlockSpec((rows, 1), lambda i: (0, 0))],
        out_specs=pl.BlockSpec((rows, seq), lambda i: (0, 0)),
        out_shape=jax.ShapeDtypeStruct((rows, seq), F32),
        compiler_params=_compiler_params(("arbitrary",)),
        name="forget_cumsum",
    )(ft2d, bias_col)


def _head_lane_masks():
    lane = lax.broadcasted_iota(jnp.int32, (1, LANES), 1)
    first = lane < HEAD_DIM
    return (first, jnp.logical_not(first))


def _split_heads(q2, masks):
    zero = jnp.zeros_like(q2)
    return tuple(jnp.where(m, q2, zero) for m in masks)


def _fox_kernel(q_ref, k_ref, v_ref, f_ref, o_ref):
    i = pl.program_id(2)
    masks = _head_lane_masks()
    qm = _split_heads(q_ref[0], masks)
    row = lax.broadcasted_iota(jnp.int32, (Q_TILE, K_TILE), 0)
    col = lax.broadcasted_iota(jnp.int32, (Q_TILE, K_TILE), 1)
    causal = col <= row
    q0 = pl.multiple_of(i * Q_TILE, Q_TILE)
    f_base = [f_ref[0, 0, h:h + 1, pl.ds(q0, LANES)][:, 0:1] for h in range(HEADS_PER_STEP)]

    def block(j, carry, masked):
        k0 = pl.multiple_of(j * K_TILE, K_TILE)
        kj = k_ref[0, pl.ds(k0, K_TILE), :]
        vj = v_ref[0, pl.ds(k0, K_TILE), :]
        one = jnp.ones_like(vj)
        out = []
        for h in range(HEADS_PER_STEP):
            m, acc = carry[h]
            v_aug = jnp.where(masks[h], vj, one)
            s = lax.dot_general(qm[h], kj, _NT, preferred_element_type=F32)
            s = s + (f_base[h] - f_ref[0, 0, h:h + 1, pl.ds(k0, K_TILE)])
            if masked:
                s = jnp.where(causal, s, -jnp.inf)
            m_new = jnp.maximum(m, jnp.max(s, axis=1, keepdims=True))
            alpha = jnp.exp(m - m_new)
            p = jnp.exp(s - m_new)
            acc = alpha * acc + jnp.dot(p.astype(BF16), v_aug, preferred_element_type=F32)
            out.append((m_new, acc))
        return tuple(out)

    init = tuple((jnp.full((Q_TILE, 1), -jnp.inf, F32), jnp.zeros((Q_TILE, LANES), F32))
                 for _ in range(HEADS_PER_STEP))
    carry = lax.fori_loop(0, i, lambda j, c: block(j, c, False), init)
    carry = block(i, carry, True)
    o = [acc / pltpu.roll(acc, HEAD_DIM, 1) for _, acc in carry]
    o_ref[0] = jnp.where(masks[0], o[0], o[1])


def _sb_kernel(q_ref, k_ref, v_ref, o_ref):
    i = pl.program_id(2)
    masks = _head_lane_masks()
    qm = _split_heads(q_ref[0], masks)
    row = lax.broadcasted_iota(jnp.int32, (Q_TILE, K_TILE), 0)
    col = lax.broadcasted_iota(jnp.int32, (Q_TILE, K_TILE), 1)
    strict = col < row
    tr = lax.broadcasted_iota(jnp.int32, (K_TILE, K_TILE), 0)
    tc = lax.broadcasted_iota(jnp.int32, (K_TILE, K_TILE), 1)
    later = (tr > tc).astype(BF16)

    def block(j, carry, masked):
        k0 = pl.multiple_of(j * K_TILE, K_TILE)
        kj = k_ref[0, pl.ds(k0, K_TILE), :]
        vj = v_ref[0, pl.ds(k0, K_TILE), :]
        out = []
        for h in range(HEADS_PER_STEP):
            r_sum, acc = carry[h]
            z = lax.dot_general(qm[h], kj, _NT, preferred_element_type=F32)
            softplus = jnp.log1p(jnp.exp(-jnp.abs(z)))
            log_beta = jnp.minimum(z, 0.0) - softplus
            log_1m = jnp.minimum(-z, 0.0) - softplus
            if masked:
                log_1m = jnp.where(strict, log_1m, 0.0)
            hi = log_1m.astype(BF16)
            lo = (log_1m - hi.astype(F32)).astype(BF16)
            suffix = (jnp.dot(hi, later, preferred_element_type=F32)
                      + jnp.dot(lo, later, preferred_element_type=F32))
            w = jnp.exp(log_beta + suffix)
            if masked:
                w = jnp.where(strict, w, 0.0)
            pv = jnp.dot(w.astype(BF16), vj, preferred_element_type=F32)
            acc = acc + jnp.exp(r_sum) * pv
            r_sum = r_sum + suffix[:, 0:1] + log_1m[:, 0:1]
            out.append((r_sum, acc))
        return tuple(out)

    init = tuple((jnp.zeros((Q_TILE, 1), F32), jnp.zeros((Q_TILE, LANES), F32))
                 for _ in range(HEADS_PER_STEP))
    carry = block(i, init, True)
    carry = lax.fori_loop(0, i, lambda t, c: block(i - 1 - t, c, False), carry)
    o_ref[0] = jnp.where(masks[0], carry[0][1], carry[1][1])


def _attention(kernel, qkv, extra, *, q_blk, k_blk, v_blk, out_width, name):
    batch, seq, _ = qkv.shape
    n_pairs = out_width // LANES
    in_specs = [
        pl.BlockSpec((1, Q_TILE, LANES), lambda b, p, i: (b, i, q_blk + p)),
        pl.BlockSpec((1, seq, LANES), lambda b, p, i: (b, 0, k_blk + p)),
        pl.BlockSpec((1, seq, LANES), lambda b, p, i: (b, 0, v_blk + p)),
    ]
    args = [qkv, qkv, qkv]
    for a in extra:
        in_specs.append(pl.BlockSpec((1, 1) + a.shape[2:], lambda b, p, i: (b, p, 0, 0)))
        args.append(a)
    return pl.pallas_call(
        kernel,
        grid=(batch, n_pairs, seq // Q_TILE),
        in_specs=in_specs,
        out_specs=pl.BlockSpec((1, Q_TILE, LANES), lambda b, p, i: (b, i, p)),
        out_shape=jax.ShapeDtypeStruct((batch, seq, out_width), F32),
        compiler_params=_compiler_params(("parallel", "parallel", "arbitrary")),
        name=name,
    )(*args)


def _out_proj_kernel(of_ref, os_ref, x_ref, gf_ref, gs_ref, w_ref, o_ref):
    a = _rmsnorm_bf16(of_ref[...], gf_ref[...])
    b = _rmsnorm_bf16(os_ref[...], gs_ref[...])
    wf = a.shape[1]
    y = (jnp.dot(a, w_ref[:wf, :], preferred_element_type=F32)
         + jnp.dot(b, w_ref[wf:, :], preferred_element_type=F32))
    o_ref[...] = x_ref[...] + y


def _out_proj(o_fox, o_sb, x2d, g_fox, g_sb, w):
    n_rows, d = x2d.shape
    wf, ws = o_fox.shape[1], o_sb.shape[1]
    return pl.pallas_call(
        _out_proj_kernel,
        grid=(n_rows // ROW_TILE,),
        in_specs=[
            pl.BlockSpec((ROW_TILE, wf), lambda i: (i, 0)),
            pl.BlockSpec((ROW_TILE, ws), lambda i: (i, 0)),
            pl.BlockSpec((ROW_TILE, d), lambda i: (i, 0)),
            pl.BlockSpec((1, wf), lambda i: (0, 0)),
            pl.BlockSpec((1, ws), lambda i: (0, 0)),
            pl.BlockSpec((wf + ws, d), lambda i: (0, 0)),
        ],
        out_specs=pl.BlockSpec((ROW_TILE, d), lambda i: (i, 0)),
        out_shape=jax.ShapeDtypeStruct((n_rows, d), F32),
        compiler_params=_compiler_params(("parallel",)),
        name="out_proj",
    )(o_fox, o_sb, x2d, g_fox, g_sb, w)


def _conv_ffn_kernel(x_ref, halo_ref, g_ref, wup_ref, cw_ref, cb_ref, wdn_ref, gfin_ref,
                     o_ref, acc_ref, *, tiles_per_seq, d_ff):
    i = pl.program_id(0)
    x = x_ref[...]
    g = g_ref[...]
    h = _rmsnorm_bf16(x, g)
    h_halo = _rmsnorm_bf16(halo_ref[...], g)
    h_halo = jnp.where(i % tiles_per_seq == 0, jnp.zeros_like(h_halo), h_halo)
    h_ext = jnp.concatenate([h_halo, h], axis=0)

    def conv(u_ext, cols):
        out = cb_ref[:, cols] + cw_ref[CONV_WIDTH - 1:CONV_WIDTH, cols] * u_ext[BF16_SUBLANES:]
        for back in range(1, CONV_WIDTH):
            shifted = pltpu.roll(u_ext, back, 0)[BF16_SUBLANES:]
            tap = CONV_WIDTH - 1 - back
            out = out + cw_ref[tap:tap + 1, cols] * shifted
        return out

    for c in range(d_ff // FFN_CHUNK):
        gate_cols = slice(c * FFN_CHUNK, (c + 1) * FFN_CHUNK)
        val_cols = slice(d_ff + c * FFN_CHUNK, d_ff + (c + 1) * FFN_CHUNK)
        gate = conv(jnp.dot(h_ext, wup_ref[:, gate_cols], preferred_element_type=F32), gate_cols)
        val = conv(jnp.dot(h_ext, wup_ref[:, val_cols], preferred_element_type=F32), val_cols)
        act = (gate / (1.0 + jnp.exp(-gate)) * val).astype(BF16)
        y = jnp.dot(act, wdn_ref[gate_cols, :], preferred_element_type=F32)
        if c == 0:
            acc_ref[...] = y
        else:
            acc_ref[...] += y

    x2 = x + acc_ref[...]
    ms = jnp.mean(x2 * x2, axis=-1, keepdims=True)
    o_ref[...] = x2 * lax.rsqrt(ms + EPS) * gfin_ref[...]


def _conv_ffn(x2d, g, w_up, conv_w, conv_b, w_down, g_final, *, seq):
    n_rows, d = x2d.shape
    d_ff = w_down.shape[0]
    tiles_per_seq = seq // ROW_TILE
    halo_blocks_per_tile = ROW_TILE // BF16_SUBLANES
    return pl.pallas_call(
        functools.partial(_conv_ffn_kernel, tiles_per_seq=tiles_per_seq, d_ff=d_ff),
        grid=(n_rows // ROW_TILE,),
        in_specs=[
            pl.BlockSpec((ROW_TILE, d), lambda i: (i, 0)),
            pl.BlockSpec((BF16_SUBLANES, d),
                         lambda i: (jnp.maximum(i * halo_blocks_per_tile - 1, 0), 0)),
            pl.BlockSpec((1, d), lambda i: (0, 0)),
            pl.BlockSpec((d, 2 * d_ff), lambda i: (0, 0)),
            pl.BlockSpec((CONV_WIDTH, 2 * d_ff), lambda i: (0, 0)),
            pl.BlockSpec((1, 2 * d_ff), lambda i: (0, 0)),
            pl.BlockSpec((d_ff, d), lambda i: (0, 0)),
            pl.BlockSpec((1, d), lambda i: (0, 0)),
        ],
        out_specs=pl.BlockSpec((ROW_TILE, d), lambda i: (i, 0)),
        out_shape=jax.ShapeDtypeStruct((n_rows, d), F32),
        scratch_shapes=[pltpu.VMEM((ROW_TILE, d), F32)],
        compiler_params=_compiler_params(("parallel",)),
        name="conv_ffn",
    )(x2d, x2d, g, w_up, conv_w, conv_b, w_down, g_final)


def kernel(x, attn_norm_g, w_in, forget_bias, fox_out_g, sb_out_g, w_out, ffn_norm_g, w_up,
           conv_w, conv_b, w_down, final_norm_g):
    batch, seq, d = x.shape
    depth = w_in.shape[0]
    n_fox = forget_bias.shape[1]
    fox_w = fox_out_g.shape[1]
    sb_w = sb_out_g.shape[1]
    assert seq % ROW_TILE == 0 and seq % Q_TILE == 0 and Q_TILE == K_TILE
    assert fox_w == n_fox * HEAD_DIM and n_fox % HEADS_PER_STEP == 0
    assert fox_w % PROJ_COL_CHUNK == 0 and sb_w % PROJ_COL_CHUNK == 0
    assert w_down.shape[1] % FFN_CHUNK == 0

    scale = HEAD_DIM ** -0.5
    fox_blocks = fox_w // LANES
    sb_blocks = sb_w // LANES
    q_chunks = (tuple(range(fox_w // PROJ_COL_CHUNK))
                + tuple(3 * fox_w // PROJ_COL_CHUNK + c for c in range(sb_w // PROJ_COL_CHUNK)))

    x2d = x.reshape(batch * seq, d)
    for l in range(depth):
        w_l = w_in[l]
        w_main = jnp.concatenate([w_l[:, :3 * fox_w], w_l[:, 3 * fox_w + n_fox:]], axis=1).astype(BF16)
        wft = w_l[:, 3 * fox_w:3 * fox_w + n_fox].T.astype(BF16)
        qkv, f_logit_t = _in_proj(x2d, attn_norm_g[l][None, :], w_main, wft,
                                  batch=batch, seq=seq, q_chunks=q_chunks, scale=scale)
        bias_col = jnp.tile(forget_bias[l], batch)[:, None]
        f_cum = _forget_cumsum(f_logit_t.reshape(batch * n_fox, seq), bias_col)
        f_cum = f_cum.reshape(batch, n_fox // HEADS_PER_STEP, HEADS_PER_STEP, seq)
        qkv3 = qkv.reshape(batch, seq, qkv.shape[1])

        o_fox = _attention(_fox_kernel, qkv3, [f_cum], q_blk=0, k_blk=fox_blocks,
                           v_blk=2 * fox_blocks, out_width=fox_w, name="fox_attn")
        o_sb = _attention(_sb_kernel, qkv3, [], q_blk=3 * fox_blocks,
                          k_blk=3 * fox_blocks + sb_blocks, v_blk=3 * fox_blocks + 2 * sb_blocks,
                          out_width=sb_w, name="sb_attn")

        x2d = _out_proj(o_fox.reshape(batch * seq, fox_w), o_sb.reshape(batch * seq, sb_w), x2d,
                        fox_out_g[l][None, :], sb_out_g[l][None, :], w_out[l].astype(BF16))
        last = l == depth - 1
        assert last, "depth > 1 needs an un-normalised FFN output between layers"
        x2d = _conv_ffn(x2d, ffn_norm_g[l][None, :], w_up[l].astype(BF16), conv_w[l],
                        conv_b[l][None, :], w_down[l].astype(BF16), final_norm_g[None, :], seq=seq)
    return x2d.reshape(batch, seq, d)
```

```python
import functools

import jax
import jax.numpy as jnp
from jax import lax
from jax.experimental import pallas as pl
from jax.experimental.pallas import tpu as pltpu

HEAD_DIM = 64
EPS = 1e-6
CONV_WIDTH = 3

LANES = 128
HEADS_PER_STEP = LANES // HEAD_DIM
BF16_SUBLANES = 16
VMEM_LIMIT_BYTES = 56 * 1024 * 1024

ROW_TILE = 512
PROJ_COL_CHUNK = 512
ATT_TILE = 512
CUMSUM_CHUNK = 256
FFN_CHUNK = 256

F32 = jnp.float32
BF16 = jnp.bfloat16
_NT = (((1,), (1,)), ((), ()))
LOG2_E = 1.4426950408889634


def _compiler_params(semantics):
    return pltpu.CompilerParams(dimension_semantics=semantics,
                                vmem_limit_bytes=VMEM_LIMIT_BYTES)


def _rmsnorm_bf16(x, g):
    ms = jnp.mean(x * x, axis=-1, keepdims=True)
    return (x * lax.rsqrt(ms + EPS) * g).astype(BF16)


def _log_sigmoid(x):
    return jnp.minimum(x, 0.0) - jnp.log1p(jnp.exp(-jnp.abs(x)))


def _in_proj_kernel(x_ref, g_ref, w_ref, wft_ref, qkv_ref, ft_ref, *, q_chunks, scale):
    h = _rmsnorm_bf16(x_ref[...], g_ref[...])
    n_cols = w_ref.shape[1]
    for c in range(n_cols // PROJ_COL_CHUNK):
        sl = slice(c * PROJ_COL_CHUNK, (c + 1) * PROJ_COL_CHUNK)
        y = jnp.dot(h, w_ref[:, sl], preferred_element_type=F32)
        if c in q_chunks:
            y = y * scale
        qkv_ref[:, sl] = y.astype(BF16)
    ft_ref[0] = lax.dot_general(wft_ref[...], h, _NT, preferred_element_type=F32)


def _in_proj(x2d, g, w, wft, *, batch, seq, q_chunks, scale):
    n_rows, d = x2d.shape
    n_cols = w.shape[1]
    n_heads = wft.shape[0]
    tiles_per_seq = seq // ROW_TILE
    return pl.pallas_call(
        functools.partial(_in_proj_kernel, q_chunks=q_chunks, scale=scale),
        grid=(n_rows // ROW_TILE,),
        in_specs=[
            pl.BlockSpec((ROW_TILE, d), lambda i: (i, 0)),
            pl.BlockSpec((1, d), lambda i: (0, 0)),
            pl.BlockSpec((d, n_cols), lambda i: (0, 0)),
            pl.BlockSpec((n_heads, d), lambda i: (0, 0)),
        ],
        out_specs=[
            pl.BlockSpec((ROW_TILE, n_cols), lambda i: (i, 0)),
            pl.BlockSpec((1, n_heads, ROW_TILE),
                         lambda i: (i // tiles_per_seq, 0, i % tiles_per_seq)),
        ],
        out_shape=[
            jax.ShapeDtypeStruct((n_rows, n_cols), BF16),
            jax.ShapeDtypeStruct((batch, n_heads, seq), F32),
        ],
        compiler_params=_compiler_params(("parallel",)),
        name="in_proj",
    )(x2d, g, w, wft)


def _forget_cumsum_kernel(ft_ref, b_ref, f_ref):
    rows, seq = ft_ref.shape
    r = lax.broadcasted_iota(jnp.int32, (CUMSUM_CHUNK, CUMSUM_CHUNK), 0)
    c = lax.broadcasted_iota(jnp.int32, (CUMSUM_CHUNK, CUMSUM_CHUNK), 1)
    upper = (r <= c).astype(F32)
    carry = jnp.zeros((rows, 1), F32)
    for ci in range(seq // CUMSUM_CHUNK):
        sl = slice(ci * CUMSUM_CHUNK, (ci + 1) * CUMSUM_CHUNK)
        log_f = _log_sigmoid(ft_ref[:, sl] + b_ref[...])
        local = jnp.dot(log_f, upper, precision=lax.Precision.HIGHEST,
                        preferred_element_type=F32)
        f_ref[:, sl] = (local + carry) * LOG2_E
        carry = carry + local[:, CUMSUM_CHUNK - 1:CUMSUM_CHUNK]


def _forget_cumsum(ft2d, bias_col):
    rows, seq = ft2d.shape
    return pl.pallas_call(
        _forget_cumsum_kernel,
        grid=(1,),
        in_specs=[pl.BlockSpec((rows, seq), lambda i: (0, 0)),
                  pl.BlockSpec((rows, 1), lambda i: (0, 0))],
        out_specs=pl.BlockSpec((rows, seq), lambda i: (0, 0)),
        out_shape=jax.ShapeDtypeStruct((rows, seq), F32),
        compiler_params=_compiler_params(("arbitrary",)),
        name="forget_cumsum",
    )(ft2d, bias_col)


def _head_lane_masks():
    lane = lax.broadcasted_iota(jnp.int32, (1, LANES), 1)
    first = lane < HEAD_DIM
    return (first, jnp.logical_not(first))


def _split_heads(q2, masks):
    zero = jnp.zeros_like(q2)
    return tuple(jnp.where(m, q2, zero) for m in masks)


def _fox_kernel(q_ref, k_ref, v_ref, f_ref, o_ref):
    i = pl.program_id(2)
    heads = range(HEADS_PER_STEP)
    masks = _head_lane_masks()
    qm = _split_heads(q_ref[0], masks)
    row = lax.broadcasted_iota(jnp.int32, (ATT_TILE, ATT_TILE), 0)
    col = lax.broadcasted_iota(jnp.int32, (ATT_TILE, ATT_TILE), 1)
    causal = col <= row
    q0 = pl.multiple_of(i * ATT_TILE, ATT_TILE)
    f_base = [f_ref[0, 0, h:h + 1, pl.ds(q0, LANES)][:, 0:1] for h in heads]

    def scores(j, h):
        k0 = pl.multiple_of(j * ATT_TILE, ATT_TILE)
        s = lax.dot_general(qm[h], k_ref[0, pl.ds(k0, ATT_TILE), :], _NT,
                            preferred_element_type=F32)
        return s + (f_base[h] - f_ref[0, 0, h:h + 1, pl.ds(k0, ATT_TILE)])

    def update(j, s, state, h, masked):
        m, acc = state
        k0 = pl.multiple_of(j * ATT_TILE, ATT_TILE)
        vj = v_ref[0, pl.ds(k0, ATT_TILE), :]
        v_aug = jnp.where(masks[h], vj, jnp.ones_like(vj))
        if masked:
            s = jnp.where(causal, s, -jnp.inf)
        m_new = jnp.maximum(m, jnp.max(s, axis=1, keepdims=True))
        alpha = jnp.exp2(m - m_new)
        p = jnp.exp2(s - m_new)
        acc = alpha * acc + jnp.dot(p.astype(BF16), v_aug, preferred_element_type=F32)
        return m_new, acc

    def body(j, carry):
        s_cur, state = carry
        s_next = tuple(scores(j + 1, h) for h in heads)
        state = tuple(update(j, s_cur[h], state[h], h, False) for h in heads)
        return s_next, state

    state = tuple((jnp.full((ATT_TILE, 1), -jnp.inf, F32), jnp.zeros((ATT_TILE, LANES), F32))
                  for _ in heads)
    s_first = tuple(scores(0, h) for h in heads)
    s_diag, state = lax.fori_loop(0, i, body, (s_first, state))
    state = tuple(update(i, s_diag[h], state[h], h, True) for h in heads)
    o = [acc / pltpu.roll(acc, HEAD_DIM, 1) for _, acc in state]
    o_ref[0] = jnp.where(masks[0], o[0], o[1])


def _sb_kernel(q_ref, k_ref, v_ref, o_ref):
    i = pl.program_id(2)
    heads = range(HEADS_PER_STEP)
    masks = _head_lane_masks()
    qm = _split_heads(q_ref[0], masks)
    row = lax.broadcasted_iota(jnp.int32, (ATT_TILE, ATT_TILE), 0)
    col = lax.broadcasted_iota(jnp.int32, (ATT_TILE, ATT_TILE), 1)
    strict = col < row
    later = (row > col).astype(BF16)

    def scores(j, h):
        k0 = pl.multiple_of(j * ATT_TILE, ATT_TILE)
        return lax.dot_general(qm[h], k_ref[0, pl.ds(k0, ATT_TILE), :], _NT,
                               preferred_element_type=F32)

    def suffix_sums(z, masked):
        softplus = jnp.log2(1.0 + jnp.exp2(-jnp.abs(z)))
        log_beta = jnp.minimum(z, 0.0) - softplus
        log_1m = jnp.minimum(-z, 0.0) - softplus
        if masked:
            log_1m = jnp.where(strict, log_1m, 0.0)
        hi = log_1m.astype(BF16)
        lo = (log_1m - hi.astype(F32)).astype(BF16)
        suffix = (jnp.dot(hi, later, preferred_element_type=F32)
                  + jnp.dot(lo, later, preferred_element_type=F32))
        return log_beta, suffix, suffix[:, 0:1] + log_1m[:, 0:1]

    def accumulate(j, log_beta, suffix, block_sum, state, masked):
        r_sum, acc = state
        k0 = pl.multiple_of(j * ATT_TILE, ATT_TILE)
        w = jnp.exp2(log_beta + suffix)
        if masked:
            w = jnp.where(strict, w, 0.0)
        pv = jnp.dot(w.astype(BF16), v_ref[0, pl.ds(k0, ATT_TILE), :],
                     preferred_element_type=F32)
        return r_sum + block_sum, acc + jnp.exp2(r_sum) * pv

    def update(j, z, state, masked):
        parts = [suffix_sums(z[h], masked) for h in heads]
        return tuple(accumulate(j, *parts[h], state[h], masked) for h in heads)

    def body(t, carry):
        z_cur, state = carry
        j = i - 1 - t
        z_next = tuple(scores(jnp.maximum(j - 1, 0), h) for h in heads)
        return z_next, update(j, z_cur, state, False)

    state = tuple((jnp.zeros((ATT_TILE, 1), F32), jnp.zeros((ATT_TILE, LANES), F32))
                  for _ in heads)
    z_diag = tuple(scores(i, h) for h in heads)
    z_prev = tuple(scores(jnp.maximum(i - 1, 0), h) for h in heads)
    state = update(i, z_diag, state, True)
    _, state = lax.fori_loop(0, i, body, (z_prev, state))
    o_ref[0] = jnp.where(masks[0], state[0][1], state[1][1])


def _attention(kernel, qkv, extra, *, q_blk, k_blk, v_blk, out_width, name):
    batch, seq, _ = qkv.shape
    n_pairs = out_width // LANES
    in_specs = [
        pl.BlockSpec((1, ATT_TILE, LANES), lambda b, p, i: (b, i, q_blk + p)),
        pl.BlockSpec((1, seq, LANES), lambda b, p, i: (b, 0, k_blk + p)),
        pl.BlockSpec((1, seq, LANES), lambda b, p, i: (b, 0, v_blk + p)),
    ]
    args = [qkv, qkv, qkv]
    for a in extra:
        in_specs.append(pl.BlockSpec((1, 1) + a.shape[2:], lambda b, p, i: (b, p, 0, 0)))
        args.append(a)
    return pl.pallas_call(
        kernel,
        grid=(batch, n_pairs, seq // ATT_TILE),
        in_specs=in_specs,
        out_specs=pl.BlockSpec((1, ATT_TILE, LANES), lambda b, p, i: (b, i, p)),
        out_shape=jax.ShapeDtypeStruct((batch, seq, out_width), F32),
        compiler_params=_compiler_params(("parallel", "parallel", "arbitrary")),
        name=name,
    )(*args)


def _out_proj_kernel(of_ref, os_ref, x_ref, gf_ref, gs_ref, w_ref, o_ref):
    a = _rmsnorm_bf16(of_ref[...], gf_ref[...])
    b = _rmsnorm_bf16(os_ref[...], gs_ref[...])
    wf = a.shape[1]
    y = (jnp.dot(a, w_ref[:wf, :], preferred_element_type=F32)
         + jnp.dot(b, w_ref[wf:, :], preferred_element_type=F32))
    o_ref[...] = x_ref[...] + y


def _out_proj(o_fox, o_sb, x2d, g_fox, g_sb, w):
    n_rows, d = x2d.shape
    wf, ws = o_fox.shape[1], o_sb.shape[1]
    return pl.pallas_call(
        _out_proj_kernel,
        grid=(n_rows // ROW_TILE,),
        in_specs=[
            pl.BlockSpec((ROW_TILE, wf), lambda i: (i, 0)),
            pl.BlockSpec((ROW_TILE, ws), lambda i: (i, 0)),
            pl.BlockSpec((ROW_TILE, d), lambda i: (i, 0)),
            pl.BlockSpec((1, wf), lambda i: (0, 0)),
            pl.BlockSpec((1, ws), lambda i: (0, 0)),
            pl.BlockSpec((wf + ws, d), lambda i: (0, 0)),
        ],
        out_specs=pl.BlockSpec((ROW_TILE, d), lambda i: (i, 0)),
        out_shape=jax.ShapeDtypeStruct((n_rows, d), F32),
        compiler_params=_compiler_params(("parallel",)),
        name="out_proj",
    )(o_fox, o_sb, x2d, g_fox, g_sb, w)


def _conv_ffn_kernel(x_ref, halo_ref, g_ref, wup_ref, cw_ref, cb_ref, wdn_ref, gfin_ref,
                     o_ref, acc_ref, *, tiles_per_seq, d_ff):
    i = pl.program_id(0)
    x = x_ref[...]
    g = g_ref[...]
    h = _rmsnorm_bf16(x, g)
    h_halo = _rmsnorm_bf16(halo_ref[...], g)
    h_halo = jnp.where(i % tiles_per_seq == 0, jnp.zeros_like(h_halo), h_halo)
    h_ext = jnp.concatenate([h_halo, h], axis=0)

    def conv(u_ext, cols):
        out = cb_ref[:, cols] + cw_ref[CONV_WIDTH - 1:CONV_WIDTH, cols] * u_ext[BF16_SUBLANES:]
        for back in range(1, CONV_WIDTH):
            shifted = pltpu.roll(u_ext, back, 0)[BF16_SUBLANES:]
            tap = CONV_WIDTH - 1 - back
            out = out + cw_ref[tap:tap + 1, cols] * shifted
        return out

    for c in range(d_ff // FFN_CHUNK):
        gate_cols = slice(c * FFN_CHUNK, (c + 1) * FFN_CHUNK)
        val_cols = slice(d_ff + c * FFN_CHUNK, d_ff + (c + 1) * FFN_CHUNK)
        gate = conv(jnp.dot(h_ext, wup_ref[:, gate_cols], preferred_element_type=F32), gate_cols)
        val = conv(jnp.dot(h_ext, wup_ref[:, val_cols], preferred_element_type=F32), val_cols)
        act = (gate / (1.0 + jnp.exp(-gate)) * val).astype(BF16)
        y = jnp.dot(act, wdn_ref[gate_cols, :], preferred_element_type=F32)
        if c == 0:
            acc_ref[...] = y
        else:
            acc_ref[...] += y

    x2 = x + acc_ref[...]
    ms = jnp.mean(x2 * x2, axis=-1, keepdims=True)
    o_ref[...] = x2 * lax.rsqrt(ms + EPS) * gfin_ref[...]


def _conv_ffn(x2d, g, w_up, conv_w, conv_b, w_down, g_final, *, seq):
    n_rows, d = x2d.shape
    d_ff = w_down.shape[0]
    tiles_per_seq = seq // ROW_TILE
    halo_blocks_per_tile = ROW_TILE // BF16_SUBLANES
    return pl.pallas_call(
        functools.partial(_conv_ffn_kernel, tiles_per_seq=tiles_per_seq, d_ff=d_ff),
        grid=(n_rows // ROW_TILE,),
        in_specs=[
            pl.BlockSpec((ROW_TILE, d), lambda i: (i, 0)),
            pl.BlockSpec((BF16_SUBLANES, d),
                         lambda i: (jnp.maximum(i * halo_blocks_per_tile - 1, 0), 0)),
            pl.BlockSpec((1, d), lambda i: (0, 0)),
            pl.BlockSpec((d, 2 * d_ff), lambda i: (0, 0)),
            pl.BlockSpec((CONV_WIDTH, 2 * d_ff), lambda i: (0, 0)),
            pl.BlockSpec((1, 2 * d_ff), lambda i: (0, 0)),
            pl.BlockSpec((d_ff, d), lambda i: (0, 0)),
            pl.BlockSpec((1, d), lambda i: (0, 0)),
        ],
        out_specs=pl.BlockSpec((ROW_TILE, d), lambda i: (i, 0)),
        out_shape=jax.ShapeDtypeStruct((n_rows, d), F32),
        scratch_shapes=[pltpu.VMEM((ROW_TILE, d), F32)],
        compiler_params=_compiler_params(("parallel",)),
        name="conv_ffn",
    )(x2d, x2d, g, w_up, conv_w, conv_b, w_down, g_final)


def kernel(x, attn_norm_g, w_in, forget_bias, fox_out_g, sb_out_g, w_out, ffn_norm_g, w_up,
           conv_w, conv_b, w_down, final_norm_g):
    batch, seq, d = x.shape
    depth = w_in.shape[0]
    n_fox = forget_bias.shape[1]
    fox_w = fox_out_g.shape[1]
    sb_w = sb_out_g.shape[1]
    assert seq % ROW_TILE == 0 and seq % ATT_TILE == 0
    assert fox_w == n_fox * HEAD_DIM and n_fox % HEADS_PER_STEP == 0
    assert fox_w % PROJ_COL_CHUNK == 0 and sb_w % PROJ_COL_CHUNK == 0
    assert w_down.shape[1] % FFN_CHUNK == 0

    scale = LOG2_E * HEAD_DIM ** -0.5
    fox_blocks = fox_w // LANES
    sb_blocks = sb_w // LANES
    q_chunks = (tuple(range(fox_w // PROJ_COL_CHUNK))
                + tuple(3 * fox_w // PROJ_COL_CHUNK + c for c in range(sb_w // PROJ_COL_CHUNK)))

    x2d = x.reshape(batch * seq, d)
    for l in range(depth):
        w_l = w_in[l]
        w_main = jnp.concatenate([w_l[:, :3 * fox_w], w_l[:, 3 * fox_w + n_fox:]], axis=1).astype(BF16)
        wft = w_l[:, 3 * fox_w:3 * fox_w + n_fox].T.astype(BF16)
        qkv, f_logit_t = _in_proj(x2d, attn_norm_g[l][None, :], w_main, wft,
                                  batch=batch, seq=seq, q_chunks=q_chunks, scale=scale)
        bias_col = jnp.tile(forget_bias[l], batch)[:, None]
        f_cum = _forget_cumsum(f_logit_t.reshape(batch * n_fox, seq), bias_col)
        f_cum = f_cum.reshape(batch, n_fox // HEADS_PER_STEP, HEADS_PER_STEP, seq)
        qkv3 = qkv.reshape(batch, seq, qkv.shape[1])

        o_fox = _attention(_fox_kernel, qkv3, [f_cum], q_blk=0, k_blk=fox_blocks,
                           v_blk=2 * fox_blocks, out_width=fox_w, name="fox_attn")
        o_sb = _attention(_sb_kernel, qkv3, [], q_blk=3 * fox_blocks,
                          k_blk=3 * fox_blocks + sb_blocks, v_blk=3 * fox_blocks + 2 * sb_blocks,
                          out_width=sb_w, name="sb_attn")

        x2d = _out_proj(o_fox.reshape(batch * seq, fox_w), o_sb.reshape(batch * seq, sb_w), x2d,
                        fox_out_g[l][None, :], sb_out_g[l][None, :], w_out[l].astype(BF16))
        last = l == depth - 1
        assert last, "depth > 1 needs an un-normalised FFN output between layers"
        x2d = _conv_ffn(x2d, ffn_norm_g[l][None, :], w_up[l].astype(BF16), conv_w[l],
                        conv_b[l][None, :], w_down[l].astype(BF16), final_norm_g[None, :], seq=seq)
    return x2d.reshape(batch, seq, d)
```

```python
import functools

import jax
import jax.numpy as jnp
from jax import lax
from jax.experimental import pallas as pl
from jax.experimental.pallas import tpu as pltpu

HEAD_DIM = 64
EPS = 1e-6
CONV_WIDTH = 3

LANES = 128
HEADS_PER_STEP = LANES // HEAD_DIM
BF16_SUBLANES = 16
VMEM_LIMIT_BYTES = 56 * 1024 * 1024

ROW_TILE = 512
PROJ_CHUNK = 512
ATT_TILE = 512
SUFFIX_BLOCK = 256
CUMSUM_CHUNK = 256
FFN_CHUNK = 256
F_PIECES = 3
F_PIECE_STRIDE = 32

F32 = jnp.float32
BF16 = jnp.bfloat16
_NT = (((1,), (1,)), ((), ()))
LOG2_E = 1.4426950408889634


def _compiler_params(semantics):
    return pltpu.CompilerParams(dimension_semantics=semantics,
                                vmem_limit_bytes=VMEM_LIMIT_BYTES)


def _rmsnorm_bf16(x, g):
    ms = jnp.mean(x * x, axis=-1, keepdims=True)
    return (x * lax.rsqrt(ms + EPS) * g).astype(BF16)


def _log_sigmoid(x):
    return jnp.minimum(x, 0.0) - jnp.log1p(jnp.exp(-jnp.abs(x)))


def _in_proj_kernel(x_ref, g_ref, wk_ref, wt_ref, wft_ref, k_ref, t_ref, ft_ref, *,
                    q_rows, scale):
    h = _rmsnorm_bf16(x_ref[...], g_ref[...])
    for c in range(wk_ref.shape[1] // PROJ_CHUNK):
        sl = slice(c * PROJ_CHUNK, (c + 1) * PROJ_CHUNK)
        k_ref[:, sl] = jnp.dot(h, wk_ref[:, sl], preferred_element_type=F32).astype(BF16)
    for c in range(wt_ref.shape[0] // PROJ_CHUNK):
        sl = slice(c * PROJ_CHUNK, (c + 1) * PROJ_CHUNK)
        y = lax.dot_general(wt_ref[sl, :], h, _NT, preferred_element_type=F32)
        if (c + 1) * PROJ_CHUNK <= q_rows:
            y = y * scale
        t_ref[0, sl, :] = y.astype(BF16)
    ft_ref[0] = lax.dot_general(wft_ref[...], h, _NT, preferred_element_type=F32)


def _in_proj(x2d, g, w_k, w_t, wft, *, batch, seq, q_rows, scale):
    n_rows, d = x2d.shape
    k_cols = w_k.shape[1]
    t_rows = w_t.shape[0]
    n_heads = wft.shape[0]
    tiles_per_seq = seq // ROW_TILE
    pos_block = lambda i: (i // tiles_per_seq, 0, i % tiles_per_seq)
    return pl.pallas_call(
        functools.partial(_in_proj_kernel, q_rows=q_rows, scale=scale),
        grid=(n_rows // ROW_TILE,),
        in_specs=[
            pl.BlockSpec((ROW_TILE, d), lambda i: (i, 0)),
            pl.BlockSpec((1, d), lambda i: (0, 0)),
            pl.BlockSpec((d, k_cols), lambda i: (0, 0)),
            pl.BlockSpec((t_rows, d), lambda i: (0, 0)),
            pl.BlockSpec((n_heads, d), lambda i: (0, 0)),
        ],
        out_specs=[
            pl.BlockSpec((ROW_TILE, k_cols), lambda i: (i, 0)),
            pl.BlockSpec((1, t_rows, ROW_TILE), pos_block),
            pl.BlockSpec((1, n_heads, ROW_TILE), pos_block),
        ],
        out_shape=[
            jax.ShapeDtypeStruct((n_rows, k_cols), BF16),
            jax.ShapeDtypeStruct((batch, t_rows, seq), BF16),
            jax.ShapeDtypeStruct((batch, n_heads, seq), F32),
        ],
        compiler_params=_compiler_params(("parallel",)),
        name="in_proj",
    )(x2d, g, w_k, w_t, wft)


def _forget_cumsum_kernel(ft_ref, b_ref, kf_ref):
    rows, seq = ft_ref.shape
    r = lax.broadcasted_iota(jnp.int32, (CUMSUM_CHUNK, CUMSUM_CHUNK), 0)
    c = lax.broadcasted_iota(jnp.int32, (CUMSUM_CHUNK, CUMSUM_CHUNK), 1)
    upper = (r <= c).astype(F32)
    pad = jnp.zeros((LANES - F_PIECES * F_PIECE_STRIDE, CUMSUM_CHUNK), F32)
    carry = jnp.zeros((rows, 1), F32)
    for ci in range(seq // CUMSUM_CHUNK):
        sl = slice(ci * CUMSUM_CHUNK, (ci + 1) * CUMSUM_CHUNK)
        log_f = _log_sigmoid(ft_ref[:, sl] + b_ref[...])
        local = jnp.dot(log_f, upper, precision=lax.Precision.HIGHEST,
                        preferred_element_type=F32)
        rest = (local + carry) * LOG2_E
        carry = carry + local[:, CUMSUM_CHUNK - 1:CUMSUM_CHUNK]
        pieces = []
        for _ in range(F_PIECES):
            piece = rest.astype(BF16).astype(F32)
            pieces.append(piece)
            rest = rest - piece
        kf_ref[sl, :] = jnp.concatenate(pieces + [pad], axis=0).T.astype(BF16)


def _forget_cumsum(ft2d, bias_col):
    rows, seq = ft2d.shape
    assert rows == F_PIECE_STRIDE
    return pl.pallas_call(
        _forget_cumsum_kernel,
        grid=(1,),
        in_specs=[pl.BlockSpec((rows, seq), lambda i: (0, 0)),
                  pl.BlockSpec((rows, 1), lambda i: (0, 0))],
        out_specs=pl.BlockSpec((seq, LANES), lambda i: (0, 0)),
        out_shape=jax.ShapeDtypeStruct((seq, LANES), BF16),
        compiler_params=_compiler_params(("arbitrary",)),
        name="forget_cumsum",
    )(ft2d, bias_col)


def _head_row_masks():
    sub = lax.broadcasted_iota(jnp.int32, (LANES, 1), 0)
    first = sub < HEAD_DIM
    return (first, jnp.logical_not(first))


def _key_block(ref, j):
    k0 = pl.multiple_of(j * ATT_TILE, ATT_TILE)
    return ref[0, pl.ds(k0, ATT_TILE), :]


def _value_block(ref, j):
    k0 = pl.multiple_of(j * ATT_TILE, ATT_TILE)
    return ref[0, :, pl.ds(k0, ATT_TILE)]


def _fox_kernel(qt_ref, k_ref, vt_ref, kf_ref, o_ref, s_scr):
    b, p, i = pl.program_id(0), pl.program_id(1), pl.program_id(2)
    heads = range(HEADS_PER_STEP)
    row_masks = _head_row_masks()
    qt = qt_ref[0]
    sub = lax.broadcasted_iota(jnp.int32, (LANES, 1), 0)
    f_rows_used = sub < F_PIECES * F_PIECE_STRIDE
    qpt = []
    for h in heads:
        f_row = (b * pl.num_programs(1) + p) * HEADS_PER_STEP + h
        pick = jnp.logical_and(sub % F_PIECE_STRIDE == f_row, f_rows_used)
        f_sel = jnp.broadcast_to(jnp.where(pick, -1.0, 0.0).astype(BF16), qt.shape)
        qpt.append(jnp.concatenate([jnp.where(row_masks[h], qt, jnp.zeros_like(qt)), f_sel],
                                   axis=0))
    key = lax.broadcasted_iota(jnp.int32, (ATT_TILE, ATT_TILE), 0)
    qry = lax.broadcasted_iota(jnp.int32, (ATT_TILE, ATT_TILE), 1)
    causal = key <= qry

    def issue(j, slot):
        k0 = pl.multiple_of(j * ATT_TILE, ATT_TILE)
        kp = jnp.concatenate([_key_block(k_ref, j), kf_ref[pl.ds(k0, ATT_TILE), :]], axis=1)
        for h in heads:
            s_scr[slot, h] = jnp.dot(kp, qpt[h], preferred_element_type=F32)

    def consume(j, slot, state, masked):
        vt = _value_block(vt_ref, j)
        out = []
        for h in heads:
            m, acc = state[h]

            def tile():
                s = s_scr[slot, h]
                return jnp.where(causal, s, -jnp.inf) if masked else s

            v_aug = jnp.where(row_masks[h], vt, jnp.ones_like(vt))
            m_new = jnp.maximum(m, jnp.max(tile(), axis=0, keepdims=True))
            alpha = jnp.exp2(m - m_new)
            prob = jnp.exp2(tile() - m_new).astype(BF16)
            acc = alpha * acc + jnp.dot(v_aug, prob, preferred_element_type=F32)
            out.append((m_new, acc))
        return tuple(out)

    first_slot = i % 2
    issue(0, first_slot)
    state = tuple((jnp.full((1, ATT_TILE), -jnp.inf, F32), jnp.zeros((LANES, ATT_TILE), F32))
                  for _ in heads)

    def leading_block(st):
        issue(1, 0)
        return consume(0, 1, st, False)

    state = lax.cond(first_slot == 1, leading_block, lambda st: st, state)

    def block_pair(u, st):
        j = first_slot + 2 * u
        issue(j + 1, 1)
        st = consume(j, 0, st, False)
        issue(j + 2, 0)
        return consume(j + 1, 1, st, False)

    state = lax.fori_loop(0, i // 2, block_pair, state)
    state = consume(i, 0, state, True)
    o = [acc / pltpu.roll(acc, HEAD_DIM, 0) for _, acc in state]
    o_ref[0] = jnp.where(row_masks[0], o[0], o[1]).T


def _sb_kernel(qt_ref, k_ref, vt_ref, o_ref, z_scr):
    i = pl.program_id(2)
    heads = range(HEADS_PER_STEP)
    row_masks = _head_row_masks()
    qt = qt_ref[0]
    qmt = [jnp.where(row_masks[h], qt, jnp.zeros_like(qt)) for h in heads]
    key = lax.broadcasted_iota(jnp.int32, (ATT_TILE, ATT_TILE), 0)
    qry = lax.broadcasted_iota(jnp.int32, (ATT_TILE, ATT_TILE), 1)
    strict = key < qry
    tr = lax.broadcasted_iota(jnp.int32, (SUFFIX_BLOCK, SUFFIX_BLOCK), 0)
    tc = lax.broadcasted_iota(jnp.int32, (SUFFIX_BLOCK, SUFFIX_BLOCK), 1)
    not_before = (tc >= tr).astype(BF16)
    n_sub = ATT_TILE // SUFFIX_BLOCK

    def issue(j, slot):
        kj = _key_block(k_ref, j)
        for h in heads:
            z_scr[slot, h] = jnp.dot(kj, qmt[h], preferred_element_type=F32)

    def log_weights(slot, h, masked):
        z = z_scr[slot, h]
        cost = jnp.maximum(z, 0.0) + jnp.log2(1.0 + jnp.exp2(-jnp.abs(z)))
        if masked:
            cost = jnp.where(strict, cost, 0.0)
        below = jnp.zeros((1, ATT_TILE), F32)
        parts = [None] * n_sub
        for u in reversed(range(n_sub)):
            x = cost[u * SUFFIX_BLOCK:(u + 1) * SUFFIX_BLOCK]
            hi = x.astype(BF16)
            lo = (x - hi.astype(F32)).astype(BF16)
            inclusive = (jnp.dot(not_before, hi, preferred_element_type=F32)
                         + jnp.dot(not_before, lo, preferred_element_type=F32))
            parts[u] = inclusive + below
            below = below + inclusive[0:1, :]
        return z_scr[slot, h] - jnp.concatenate(parts, axis=0), below

    def consume(j, slot, state, masked):
        vt = _value_block(vt_ref, j)
        logs = [log_weights(slot, h, masked) for h in heads]
        out = []
        for h in heads:
            r_cost, acc = state[h]
            log_w, block_cost = logs[h]
            w = jnp.exp2(log_w)
            if masked:
                w = jnp.where(strict, w, 0.0)
            pv = jnp.dot(vt, w.astype(BF16), preferred_element_type=F32)
            out.append((r_cost + block_cost, acc + jnp.exp2(-r_cost) * pv))
        return tuple(out)

    issue(i, 0)
    issue(jnp.maximum(i - 1, 0), 1)
    state = tuple((jnp.zeros((1, ATT_TILE), F32), jnp.zeros((LANES, ATT_TILE), F32))
                  for _ in heads)
    state = consume(i, 0, state, True)

    def block_pair(u, st):
        j = i - 1 - 2 * u
        issue(j - 1, 0)
        st = consume(j, 1, st, False)
        issue(jnp.maximum(j - 2, 0), 1)
        return consume(j - 1, 0, st, False)

    state = lax.fori_loop(0, i // 2, block_pair, state)
    state = lax.cond(i % 2 == 1, lambda st: consume(0, 1, st, False), lambda st: st, state)
    o_ref[0] = jnp.where(row_masks[0], state[0][1], state[1][1]).T


def _attention(kernel, k3, t3, extra, *, k_blk, q_blk, v_blk, out_width, name):
    batch, seq, _ = k3.shape
    n_pairs = out_width // LANES
    in_specs = [
        pl.BlockSpec((1, LANES, ATT_TILE), lambda b, p, i: (b, q_blk + p, i)),
        pl.BlockSpec((1, seq, LANES), lambda b, p, i: (b, 0, k_blk + p)),
        pl.BlockSpec((1, LANES, seq), lambda b, p, i: (b, v_blk + p, 0)),
    ]
    args = [t3, k3, t3]
    for a in extra:
        in_specs.append(pl.BlockSpec(a.shape, lambda b, p, i: (0, 0)))
        args.append(a)
    return pl.pallas_call(
        kernel,
        grid=(batch, n_pairs, seq // ATT_TILE),
        in_specs=in_specs,
        out_specs=pl.BlockSpec((1, ATT_TILE, LANES), lambda b, p, i: (b, i, p)),
        out_shape=jax.ShapeDtypeStruct((batch, seq, out_width), F32),
        scratch_shapes=[pltpu.VMEM((2, HEADS_PER_STEP, ATT_TILE, ATT_TILE), F32)],
        compiler_params=_compiler_params(("parallel", "parallel", "arbitrary")),
        name=name,
    )(*args)


def _out_proj_kernel(of_ref, os_ref, x_ref, gf_ref, gs_ref, w_ref, o_ref):
    a = _rmsnorm_bf16(of_ref[...], gf_ref[...])
    b = _rmsnorm_bf16(os_ref[...], gs_ref[...])
    wf = a.shape[1]
    y = (jnp.dot(a, w_ref[:wf, :], preferred_element_type=F32)
         + jnp.dot(b, w_ref[wf:, :], preferred_element_type=F32))
    o_ref[...] = x_ref[...] + y


def _out_proj(o_fox, o_sb, x2d, g_fox, g_sb, w):
    n_rows, d = x2d.shape
    wf, ws = o_fox.shape[1], o_sb.shape[1]
    return pl.pallas_call(
        _out_proj_kernel,
        grid=(n_rows // ROW_TILE,),
        in_specs=[
            pl.BlockSpec((ROW_TILE, wf), lambda i: (i, 0)),
            pl.BlockSpec((ROW_TILE, ws), lambda i: (i, 0)),
            pl.BlockSpec((ROW_TILE, d), lambda i: (i, 0)),
            pl.BlockSpec((1, wf), lambda i: (0, 0)),
            pl.BlockSpec((1, ws), lambda i: (0, 0)),
            pl.BlockSpec((wf + ws, d), lambda i: (0, 0)),
        ],
        out_specs=pl.BlockSpec((ROW_TILE, d), lambda i: (i, 0)),
        out_shape=jax.ShapeDtypeStruct((n_rows, d), F32),
        compiler_params=_compiler_params(("parallel",)),
        name="out_proj",
    )(o_fox, o_sb, x2d, g_fox, g_sb, w)


def _conv_ffn_kernel(x_ref, halo_ref, g_ref, wup_ref, cw_ref, cb_ref, wdn_ref, gfin_ref,
                     o_ref, acc_ref, *, tiles_per_seq, d_ff):
    i = pl.program_id(0)
    x = x_ref[...]
    g = g_ref[...]
    h = _rmsnorm_bf16(x, g)
    h_halo = _rmsnorm_bf16(halo_ref[...], g)
    h_halo = jnp.where(i % tiles_per_seq == 0, jnp.zeros_like(h_halo), h_halo)
    h_ext = jnp.concatenate([h_halo, h], axis=0)

    def conv(u_ext, cols):
        out = cb_ref[:, cols] + cw_ref[CONV_WIDTH - 1:CONV_WIDTH, cols] * u_ext[BF16_SUBLANES:]
        for back in range(1, CONV_WIDTH):
            shifted = pltpu.roll(u_ext, back, 0)[BF16_SUBLANES:]
            tap = CONV_WIDTH - 1 - back
            out = out + cw_ref[tap:tap + 1, cols] * shifted
        return out

    for c in range(d_ff // FFN_CHUNK):
        gate_cols = slice(c * FFN_CHUNK, (c + 1) * FFN_CHUNK)
        val_cols = slice(d_ff + c * FFN_CHUNK, d_ff + (c + 1) * FFN_CHUNK)
        gate = conv(jnp.dot(h_ext, wup_ref[:, gate_cols], preferred_element_type=F32), gate_cols)
        val = conv(jnp.dot(h_ext, wup_ref[:, val_cols], preferred_element_type=F32), val_cols)
        act = (gate / (1.0 + jnp.exp(-gate)) * val).astype(BF16)
        y = jnp.dot(act, wdn_ref[gate_cols, :], preferred_element_type=F32)
        if c == 0:
            acc_ref[...] = y
        else:
            acc_ref[...] += y

    x2 = x + acc_ref[...]
    ms = jnp.mean(x2 * x2, axis=-1, keepdims=True)
    o_ref[...] = x2 * lax.rsqrt(ms + EPS) * gfin_ref[...]


def _conv_ffn(x2d, g, w_up, conv_w, conv_b, w_down, g_final, *, seq):
    n_rows, d = x2d.shape
    d_ff = w_down.shape[0]
    tiles_per_seq = seq // ROW_TILE
    halo_blocks_per_tile = ROW_TILE // BF16_SUBLANES
    return pl.pallas_call(
        functools.partial(_conv_ffn_kernel, tiles_per_seq=tiles_per_seq, d_ff=d_ff),
        grid=(n_rows // ROW_TILE,),
        in_specs=[
            pl.BlockSpec((ROW_TILE, d), lambda i: (i, 0)),
            pl.BlockSpec((BF16_SUBLANES, d),
                         lambda i: (jnp.maximum(i * halo_blocks_per_tile - 1, 0), 0)),
            pl.BlockSpec((1, d), lambda i: (0, 0)),
            pl.BlockSpec((d, 2 * d_ff), lambda i: (0, 0)),
            pl.BlockSpec((CONV_WIDTH, 2 * d_ff), lambda i: (0, 0)),
            pl.BlockSpec((1, 2 * d_ff), lambda i: (0, 0)),
            pl.BlockSpec((d_ff, d), lambda i: (0, 0)),
            pl.BlockSpec((1, d), lambda i: (0, 0)),
        ],
        out_specs=pl.BlockSpec((ROW_TILE, d), lambda i: (i, 0)),
        out_shape=jax.ShapeDtypeStruct((n_rows, d), F32),
        scratch_shapes=[pltpu.VMEM((ROW_TILE, d), F32)],
        compiler_params=_compiler_params(("parallel",)),
        name="conv_ffn",
    )(x2d, x2d, g, w_up, conv_w, conv_b, w_down, g_final)


def kernel(x, attn_norm_g, w_in, forget_bias, fox_out_g, sb_out_g, w_out, ffn_norm_g, w_up,
           conv_w, conv_b, w_down, final_norm_g):
    batch, seq, d = x.shape
    depth = w_in.shape[0]
    n_fox = forget_bias.shape[1]
    fox_w = fox_out_g.shape[1]
    sb_w = sb_out_g.shape[1]
    assert seq % ROW_TILE == 0 and seq % ATT_TILE == 0 and ATT_TILE % SUFFIX_BLOCK == 0
    assert fox_w == n_fox * HEAD_DIM and n_fox % HEADS_PER_STEP == 0
    assert fox_w % PROJ_CHUNK == 0 and sb_w % PROJ_CHUNK == 0
    assert w_down.shape[1] % FFN_CHUNK == 0

    scale = LOG2_E * HEAD_DIM ** -0.5
    fox_blocks = fox_w // LANES
    sb_blocks = sb_w // LANES
    c_fq, c_fk, c_fv, c_fl = 0, fox_w, 2 * fox_w, 3 * fox_w
    c_sq = c_fl + n_fox
    c_sk, c_sv = c_sq + sb_w, c_sq + 2 * sb_w

    x2d = x.reshape(batch * seq, d)
    for l in range(depth):
        w_l = w_in[l]
        cols = lambda start, width: w_l[:, start:start + width]
        w_k = jnp.concatenate([cols(c_fk, fox_w), cols(c_sk, sb_w)], axis=1).astype(BF16)
        w_t = jnp.concatenate([cols(c_fq, fox_w), cols(c_sq, sb_w),
                               cols(c_fv, fox_w), cols(c_sv, sb_w)], axis=1).T.astype(BF16)
        wft = cols(c_fl, n_fox).T.astype(BF16)
        k2d, t3, f_logit_t = _in_proj(x2d, attn_norm_g[l][None, :], w_k, w_t, wft,
                                      batch=batch, seq=seq, q_rows=fox_w + sb_w, scale=scale)
        bias_col = jnp.tile(forget_bias[l], batch)[:, None]
        kf = _forget_cumsum(f_logit_t.reshape(batch * n_fox, seq), bias_col)
        k3 = k2d.reshape(batch, seq, k2d.shape[1])

        o_fox = _attention(_fox_kernel, k3, t3, [kf], k_blk=0, q_blk=0,
                           v_blk=fox_blocks + sb_blocks, out_width=fox_w, name="fox_attn")
        o_sb = _attention(_sb_kernel, k3, t3, [], k_blk=fox_blocks, q_blk=fox_blocks,
                          v_blk=2 * fox_blocks + sb_blocks, out_width=sb_w, name="sb_attn")

        x2d = _out_proj(o_fox.reshape(batch * seq, fox_w), o_sb.reshape(batch * seq, sb_w), x2d,
                        fox_out_g[l][None, :], sb_out_g[l][None, :], w_out[l].astype(BF16))
        last = l == depth - 1
        assert last, "depth > 1 needs an un-normalised FFN output between layers"
        x2d = _conv_ffn(x2d, ffn_norm_g[l][None, :], w_up[l].astype(BF16), conv_w[l],
                        conv_b[l][None, :], w_down[l].astype(BF16), final_norm_g[None, :], seq=seq)
    return x2d.reshape(batch, seq, d)
```

```python
import functools

import jax
import jax.numpy as jnp
from jax import lax
from jax.experimental import pallas as pl
from jax.experimental.pallas import tpu as pltpu

HEAD_DIM = 64
EPS = 1e-6
CONV_WIDTH = 3

LANES = 128
HEADS_PER_STEP = LANES // HEAD_DIM
BF16_SUBLANES = 16
VMEM_LIMIT_BYTES = 56 * 1024 * 1024

ROW_TILE = 512
PROJ_CHUNK = 512
ATT_TILE = 512
SUFFIX_BLOCK = 256
CUMSUM_CHUNK = 256
FFN_CHUNK = 256
F_PIECES = 3
F_PIECE_STRIDE = 32

F32 = jnp.float32
BF16 = jnp.bfloat16
_NT = (((1,), (1,)), ((), ()))
LOG2_E = 1.4426950408889634
DEAD_COST_LOG2 = 160.0


def _compiler_params(semantics):
    return pltpu.CompilerParams(dimension_semantics=semantics,
                                vmem_limit_bytes=VMEM_LIMIT_BYTES)


def _rmsnorm_bf16(x, g):
    ms = jnp.mean(x * x, axis=-1, keepdims=True)
    return (x * lax.rsqrt(ms + EPS) * g).astype(BF16)


def _log_sigmoid(x):
    return jnp.minimum(x, 0.0) - jnp.log1p(jnp.exp(-jnp.abs(x)))


def _in_proj_kernel(x_ref, g_ref, wk_ref, wt_ref, wft_ref, k_ref, t_ref, ft_ref, *,
                    q_rows, scale):
    h = _rmsnorm_bf16(x_ref[...], g_ref[...])
    for c in range(wk_ref.shape[1] // PROJ_CHUNK):
        sl = slice(c * PROJ_CHUNK, (c + 1) * PROJ_CHUNK)
        k_ref[:, sl] = jnp.dot(h, wk_ref[:, sl], preferred_element_type=F32).astype(BF16)
    for c in range(wt_ref.shape[0] // PROJ_CHUNK):
        sl = slice(c * PROJ_CHUNK, (c + 1) * PROJ_CHUNK)
        y = lax.dot_general(wt_ref[sl, :], h, _NT, preferred_element_type=F32)
        if (c + 1) * PROJ_CHUNK <= q_rows:
            y = y * scale
        t_ref[0, sl, :] = y.astype(BF16)
    ft_ref[0] = lax.dot_general(wft_ref[...], h, _NT, preferred_element_type=F32)


def _in_proj(x2d, g, w_k, w_t, wft, *, batch, seq, q_rows, scale):
    n_rows, d = x2d.shape
    k_cols = w_k.shape[1]
    t_rows = w_t.shape[0]
    n_heads = wft.shape[0]
    tiles_per_seq = seq // ROW_TILE
    pos_block = lambda i: (i // tiles_per_seq, 0, i % tiles_per_seq)
    return pl.pallas_call(
        functools.partial(_in_proj_kernel, q_rows=q_rows, scale=scale),
        grid=(n_rows // ROW_TILE,),
        in_specs=[
            pl.BlockSpec((ROW_TILE, d), lambda i: (i, 0)),
            pl.BlockSpec((1, d), lambda i: (0, 0)),
            pl.BlockSpec((d, k_cols), lambda i: (0, 0)),
            pl.BlockSpec((t_rows, d), lambda i: (0, 0)),
            pl.BlockSpec((n_heads, d), lambda i: (0, 0)),
        ],
        out_specs=[
            pl.BlockSpec((ROW_TILE, k_cols), lambda i: (i, 0)),
            pl.BlockSpec((1, t_rows, ROW_TILE), pos_block),
            pl.BlockSpec((1, n_heads, ROW_TILE), pos_block),
        ],
        out_shape=[
            jax.ShapeDtypeStruct((n_rows, k_cols), BF16),
            jax.ShapeDtypeStruct((batch, t_rows, seq), BF16),
            jax.ShapeDtypeStruct((batch, n_heads, seq), F32),
        ],
        compiler_params=_compiler_params(("parallel",)),
        name="in_proj",
    )(x2d, g, w_k, w_t, wft)


def _forget_cumsum_kernel(ft_ref, b_ref, kf_ref):
    rows, seq = ft_ref.shape
    r = lax.broadcasted_iota(jnp.int32, (CUMSUM_CHUNK, CUMSUM_CHUNK), 0)
    c = lax.broadcasted_iota(jnp.int32, (CUMSUM_CHUNK, CUMSUM_CHUNK), 1)
    upper = (r <= c).astype(F32)
    pad = jnp.zeros((LANES - F_PIECES * F_PIECE_STRIDE, CUMSUM_CHUNK), F32)
    carry = jnp.zeros((rows, 1), F32)
    for ci in range(seq // CUMSUM_CHUNK):
        sl = slice(ci * CUMSUM_CHUNK, (ci + 1) * CUMSUM_CHUNK)
        log_f = _log_sigmoid(ft_ref[:, sl] + b_ref[...])
        local = jnp.dot(log_f, upper, precision=lax.Precision.HIGHEST,
                        preferred_element_type=F32)
        rest = (local + carry) * LOG2_E
        carry = carry + local[:, CUMSUM_CHUNK - 1:CUMSUM_CHUNK]
        pieces = []
        for _ in range(F_PIECES):
            piece = rest.astype(BF16).astype(F32)
            pieces.append(piece)
            rest = rest - piece
        kf_ref[sl, :] = jnp.concatenate(pieces + [pad], axis=0).T.astype(BF16)


def _forget_cumsum(ft2d, bias_col):
    rows, seq = ft2d.shape
    assert rows == F_PIECE_STRIDE
    return pl.pallas_call(
        _forget_cumsum_kernel,
        grid=(1,),
        in_specs=[pl.BlockSpec((rows, seq), lambda i: (0, 0)),
                  pl.BlockSpec((rows, 1), lambda i: (0, 0))],
        out_specs=pl.BlockSpec((seq, LANES), lambda i: (0, 0)),
        out_shape=jax.ShapeDtypeStruct((seq, LANES), BF16),
        compiler_params=_compiler_params(("arbitrary",)),
        name="forget_cumsum",
    )(ft2d, bias_col)


def _head_row_masks():
    sub = lax.broadcasted_iota(jnp.int32, (LANES, 1), 0)
    first = sub < HEAD_DIM
    return (first, jnp.logical_not(first))


def _key_block(ref, j):
    k0 = pl.multiple_of(j * ATT_TILE, ATT_TILE)
    return ref[0, pl.ds(k0, ATT_TILE), :]


def _value_block(ref, j):
    k0 = pl.multiple_of(j * ATT_TILE, ATT_TILE)
    return ref[0, :, pl.ds(k0, ATT_TILE)]


def _fox_kernel(qt_ref, k_ref, vt_ref, kf_ref, o_ref, s_scr):
    b, p, i = pl.program_id(0), pl.program_id(1), pl.program_id(2)
    heads = range(HEADS_PER_STEP)
    row_masks = _head_row_masks()
    qt = qt_ref[0]
    sub = lax.broadcasted_iota(jnp.int32, (LANES, 1), 0)
    f_rows_used = sub < F_PIECES * F_PIECE_STRIDE
    qpt = []
    for h in heads:
        f_row = (b * pl.num_programs(1) + p) * HEADS_PER_STEP + h
        pick = jnp.logical_and(sub % F_PIECE_STRIDE == f_row, f_rows_used)
        f_sel = jnp.broadcast_to(jnp.where(pick, -1.0, 0.0).astype(BF16), qt.shape)
        qpt.append(jnp.concatenate([jnp.where(row_masks[h], qt, jnp.zeros_like(qt)), f_sel],
                                   axis=0))
    key = lax.broadcasted_iota(jnp.int32, (ATT_TILE, ATT_TILE), 0)
    qry = lax.broadcasted_iota(jnp.int32, (ATT_TILE, ATT_TILE), 1)
    causal = key <= qry

    def issue(j, slot):
        k0 = pl.multiple_of(j * ATT_TILE, ATT_TILE)
        kp = jnp.concatenate([_key_block(k_ref, j), kf_ref[pl.ds(k0, ATT_TILE), :]], axis=1)
        for h in heads:
            s_scr[slot, h] = jnp.dot(kp, qpt[h], preferred_element_type=F32)

    def consume(j, slot, state, masked):
        vt = _value_block(vt_ref, j)
        out = []
        for h in heads:
            m, acc = state[h]

            def tile():
                s = s_scr[slot, h]
                return jnp.where(causal, s, -jnp.inf) if masked else s

            v_aug = jnp.where(row_masks[h], vt, jnp.ones_like(vt))
            m_new = jnp.maximum(m, jnp.max(tile(), axis=0, keepdims=True))
            alpha = jnp.exp2(m - m_new)
            prob = jnp.exp2(tile() - m_new).astype(BF16)
            acc = alpha * acc + jnp.dot(v_aug, prob, preferred_element_type=F32)
            out.append((m_new, acc))
        return tuple(out)

    first_slot = i % 2
    issue(0, first_slot)
    state = tuple((jnp.full((1, ATT_TILE), -jnp.inf, F32), jnp.zeros((LANES, ATT_TILE), F32))
                  for _ in heads)

    def leading_block(st):
        issue(1, 0)
        return consume(0, 1, st, False)

    state = lax.cond(first_slot == 1, leading_block, lambda st: st, state)

    def block_pair(u, st):
        j = first_slot + 2 * u
        issue(j + 1, 1)
        st = consume(j, 0, st, False)
        issue(j + 2, 0)
        return consume(j + 1, 1, st, False)

    state = lax.fori_loop(0, i // 2, block_pair, state)
    state = consume(i, 0, state, True)
    o = [acc / pltpu.roll(acc, HEAD_DIM, 0) for _, acc in state]
    o_ref[0] = jnp.where(row_masks[0], o[0], o[1]).T


def _sb_kernel(qt_ref, k_ref, vt_ref, o_ref, z_scr):
    i = pl.program_id(2)
    heads = range(HEADS_PER_STEP)
    row_masks = _head_row_masks()
    qt = qt_ref[0]
    qmt = [jnp.where(row_masks[h], qt, jnp.zeros_like(qt)) for h in heads]
    key = lax.broadcasted_iota(jnp.int32, (ATT_TILE, ATT_TILE), 0)
    qry = lax.broadcasted_iota(jnp.int32, (ATT_TILE, ATT_TILE), 1)
    strict = key < qry
    tr = lax.broadcasted_iota(jnp.int32, (SUFFIX_BLOCK, SUFFIX_BLOCK), 0)
    tc = lax.broadcasted_iota(jnp.int32, (SUFFIX_BLOCK, SUFFIX_BLOCK), 1)
    not_before = (tc >= tr).astype(BF16)
    n_sub = ATT_TILE // SUFFIX_BLOCK

    def issue(j, slot):
        kj = _key_block(k_ref, j)
        for h in heads:
            z_scr[slot, h] = jnp.dot(kj, qmt[h], preferred_element_type=F32)

    def log_weights(slot, h, masked):
        z = z_scr[slot, h]
        cost = jnp.maximum(z, 0.0) + jnp.log2(1.0 + jnp.exp2(-jnp.abs(z)))
        if masked:
            cost = jnp.where(strict, cost, 0.0)
        below = jnp.zeros((1, ATT_TILE), F32)
        parts = [None] * n_sub
        for u in reversed(range(n_sub)):
            x = cost[u * SUFFIX_BLOCK:(u + 1) * SUFFIX_BLOCK]
            hi = x.astype(BF16)
            lo = (x - hi.astype(F32)).astype(BF16)
            inclusive = (jnp.dot(not_before, hi, preferred_element_type=F32)
                         + jnp.dot(not_before, lo, preferred_element_type=F32))
            parts[u] = inclusive + below
            below = below + inclusive[0:1, :]
        return z_scr[slot, h] - jnp.concatenate(parts, axis=0), below

    def consume(j, slot, state, masked):
        vt = _value_block(vt_ref, j)
        logs = [log_weights(slot, h, masked) for h in heads]
        out = []
        for h in heads:
            r_cost, acc = state[h]
            log_w, block_cost = logs[h]
            w = jnp.exp2(log_w)
            if masked:
                w = jnp.where(strict, w, 0.0)
            pv = jnp.dot(vt, w.astype(BF16), preferred_element_type=F32)
            out.append((r_cost + block_cost, acc + jnp.exp2(-r_cost) * pv))
        return tuple(out)

    issue(i, 0)
    issue(jnp.maximum(i - 1, 0), 1)
    state = tuple((jnp.zeros((1, ATT_TILE), F32), jnp.zeros((LANES, ATT_TILE), F32))
                  for _ in heads)
    state = consume(i, 0, state, True)

    def still_live(st):
        lowest = functools.reduce(jnp.minimum, [jnp.min(r_cost) for r_cost, _ in st])
        return lowest < DEAD_COST_LOG2

    def block_pair(carry):
        u, st, _ = carry
        j = i - 1 - 2 * u
        issue(j - 1, 0)
        st = consume(j, 1, st, False)

        def second(st):
            issue(jnp.maximum(j - 2, 0), 1)
            return consume(j - 1, 0, st, False)

        st = lax.cond(still_live(st), second, lambda st: st, st)
        return u + 1, st, still_live(st)

    n_pairs = i // 2
    _, state, live = lax.while_loop(lambda c: jnp.logical_and(c[0] < n_pairs, c[2]),
                                    block_pair, (jnp.int32(0), state, still_live(state)))
    state = lax.cond(jnp.logical_and(i % 2 == 1, live),
                     lambda st: consume(0, 1, st, False), lambda st: st, state)
    o_ref[0] = jnp.where(row_masks[0], state[0][1], state[1][1]).T


def _attention(kernel, k3, t3, extra, *, k_blk, q_blk, v_blk, out_width, name):
    batch, seq, _ = k3.shape
    n_pairs = out_width // LANES
    in_specs = [
        pl.BlockSpec((1, LANES, ATT_TILE), lambda b, p, i: (b, q_blk + p, i)),
        pl.BlockSpec((1, seq, LANES), lambda b, p, i: (b, 0, k_blk + p)),
        pl.BlockSpec((1, LANES, seq), lambda b, p, i: (b, v_blk + p, 0)),
    ]
    args = [t3, k3, t3]
    for a in extra:
        in_specs.append(pl.BlockSpec(a.shape, lambda b, p, i: (0, 0)))
        args.append(a)
    return pl.pallas_call(
        kernel,
        grid=(batch, n_pairs, seq // ATT_TILE),
        in_specs=in_specs,
        out_specs=pl.BlockSpec((1, ATT_TILE, LANES), lambda b, p, i: (b, i, p)),
        out_shape=jax.ShapeDtypeStruct((batch, seq, out_width), F32),
        scratch_shapes=[pltpu.VMEM((2, HEADS_PER_STEP, ATT_TILE, ATT_TILE), F32)],
        compiler_params=_compiler_params(("parallel", "parallel", "arbitrary")),
        name=name,
    )(*args)


def _out_proj_kernel(of_ref, os_ref, x_ref, gf_ref, gs_ref, w_ref, o_ref):
    a = _rmsnorm_bf16(of_ref[...], gf_ref[...])
    b = _rmsnorm_bf16(os_ref[...], gs_ref[...])
    wf = a.shape[1]
    y = (jnp.dot(a, w_ref[:wf, :], preferred_element_type=F32)
         + jnp.dot(b, w_ref[wf:, :], preferred_element_type=F32))
    o_ref[...] = x_ref[...] + y


def _out_proj(o_fox, o_sb, x2d, g_fox, g_sb, w):
    n_rows, d = x2d.shape
    wf, ws = o_fox.shape[1], o_sb.shape[1]
    return pl.pallas_call(
        _out_proj_kernel,
        grid=(n_rows // ROW_TILE,),
        in_specs=[
            pl.BlockSpec((ROW_TILE, wf), lambda i: (i, 0)),
            pl.BlockSpec((ROW_TILE, ws), lambda i: (i, 0)),
            pl.BlockSpec((ROW_TILE, d), lambda i: (i, 0)),
            pl.BlockSpec((1, wf), lambda i: (0, 0)),
            pl.BlockSpec((1, ws), lambda i: (0, 0)),
            pl.BlockSpec((wf + ws, d), lambda i: (0, 0)),
        ],
        out_specs=pl.BlockSpec((ROW_TILE, d), lambda i: (i, 0)),
        out_shape=jax.ShapeDtypeStruct((n_rows, d), F32),
        compiler_params=_compiler_params(("parallel",)),
        name="out_proj",
    )(o_fox, o_sb, x2d, g_fox, g_sb, w)


def _conv_ffn_kernel(x_ref, halo_ref, g_ref, wup_ref, cw_ref, cb_ref, wdn_ref, gfin_ref,
                     o_ref, acc_ref, *, tiles_per_seq, d_ff):
    i = pl.program_id(0)
    x = x_ref[...]
    g = g_ref[...]
    h = _rmsnorm_bf16(x, g)
    h_halo = _rmsnorm_bf16(halo_ref[...], g)
    h_halo = jnp.where(i % tiles_per_seq == 0, jnp.zeros_like(h_halo), h_halo)
    h_ext = jnp.concatenate([h_halo, h], axis=0)

    def conv(u_ext, cols):
        out = cb_ref[:, cols] + cw_ref[CONV_WIDTH - 1:CONV_WIDTH, cols] * u_ext[BF16_SUBLANES:]
        for back in range(1, CONV_WIDTH):
            shifted = pltpu.roll(u_ext, back, 0)[BF16_SUBLANES:]
            tap = CONV_WIDTH - 1 - back
            out = out + cw_ref[tap:tap + 1, cols] * shifted
        return out

    def chunk_cols(c):
        return (slice(c * FFN_CHUNK, (c + 1) * FFN_CHUNK),
                slice(d_ff + c * FFN_CHUNK, d_ff + (c + 1) * FFN_CHUNK))

    def up_proj(c):
        return tuple(jnp.dot(h_ext, wup_ref[:, cols], preferred_element_type=F32)
                     for cols in chunk_cols(c))

    n_chunks = d_ff // FFN_CHUNK
    u_next = up_proj(0)
    for c in range(n_chunks):
        gate_cols, val_cols = chunk_cols(c)
        u_gate, u_val = u_next
        if c + 1 < n_chunks:
            u_next = up_proj(c + 1)
        gate = conv(u_gate, gate_cols)
        val = conv(u_val, val_cols)
        act = (gate / (1.0 + jnp.exp(-gate)) * val).astype(BF16)
        y = jnp.dot(act, wdn_ref[gate_cols, :], preferred_element_type=F32)
        if c == 0:
            acc_ref[...] = y
        else:
            acc_ref[...] += y

    x2 = x + acc_ref[...]
    ms = jnp.mean(x2 * x2, axis=-1, keepdims=True)
    o_ref[...] = x2 * lax.rsqrt(ms + EPS) * gfin_ref[...]


def _conv_ffn(x2d, g, w_up, conv_w, conv_b, w_down, g_final, *, seq):
    n_rows, d = x2d.shape
    d_ff = w_down.shape[0]
    tiles_per_seq = seq // ROW_TILE
    halo_blocks_per_tile = ROW_TILE // BF16_SUBLANES
    return pl.pallas_call(
        functools.partial(_conv_ffn_kernel, tiles_per_seq=tiles_per_seq, d_ff=d_ff),
        grid=(n_rows // ROW_TILE,),
        in_specs=[
            pl.BlockSpec((ROW_TILE, d), lambda i: (i, 0)),
            pl.BlockSpec((BF16_SUBLANES, d),
                         lambda i: (jnp.maximum(i * halo_blocks_per_tile - 1, 0), 0)),
            pl.BlockSpec((1, d), lambda i: (0, 0)),
            pl.BlockSpec((d, 2 * d_ff), lambda i: (0, 0)),
            pl.BlockSpec((CONV_WIDTH, 2 * d_ff), lambda i: (0, 0)),
            pl.BlockSpec((1, 2 * d_ff), lambda i: (0, 0)),
            pl.BlockSpec((d_ff, d), lambda i: (0, 0)),
            pl.BlockSpec((1, d), lambda i: (0, 0)),
        ],
        out_specs=pl.BlockSpec((ROW_TILE, d), lambda i: (i, 0)),
        out_shape=jax.ShapeDtypeStruct((n_rows, d), F32),
        scratch_shapes=[pltpu.VMEM((ROW_TILE, d), F32)],
        compiler_params=_compiler_params(("parallel",)),
        name="conv_ffn",
    )(x2d, x2d, g, w_up, conv_w, conv_b, w_down, g_final)


def kernel(x, attn_norm_g, w_in, forget_bias, fox_out_g, sb_out_g, w_out, ffn_norm_g, w_up,
           conv_w, conv_b, w_down, final_norm_g):
    batch, seq, d = x.shape
    depth = w_in.shape[0]
    n_fox = forget_bias.shape[1]
    fox_w = fox_out_g.shape[1]
    sb_w = sb_out_g.shape[1]
    assert seq % ROW_TILE == 0 and seq % ATT_TILE == 0 and ATT_TILE % SUFFIX_BLOCK == 0
    assert fox_w == n_fox * HEAD_DIM and n_fox % HEADS_PER_STEP == 0
    assert fox_w % PROJ_CHUNK == 0 and sb_w % PROJ_CHUNK == 0
    assert w_down.shape[1] % FFN_CHUNK == 0

    scale = LOG2_E * HEAD_DIM ** -0.5
    fox_blocks = fox_w // LANES
    sb_blocks = sb_w // LANES
    c_fq, c_fk, c_fv, c_fl = 0, fox_w, 2 * fox_w, 3 * fox_w
    c_sq = c_fl + n_fox
    c_sk, c_sv = c_sq + sb_w, c_sq + 2 * sb_w

    x2d = x.reshape(batch * seq, d)
    for l in range(depth):
        w_l = w_in[l]
        cols = lambda start, width: w_l[:, start:start + width]
        w_k = jnp.concatenate([cols(c_fk, fox_w), cols(c_sk, sb_w)], axis=1).astype(BF16)
        w_t = jnp.concatenate([cols(c_fq, fox_w), cols(c_sq, sb_w),
                               cols(c_fv, fox_w), cols(c_sv, sb_w)], axis=1).T.astype(BF16)
        wft = cols(c_fl, n_fox).T.astype(BF16)
        k2d, t3, f_logit_t = _in_proj(x2d, attn_norm_g[l][None, :], w_k, w_t, wft,
                                      batch=batch, seq=seq, q_rows=fox_w + sb_w, scale=scale)
        bias_col = jnp.tile(forget_bias[l], batch)[:, None]
        kf = _forget_cumsum(f_logit_t.reshape(batch * n_fox, seq), bias_col)
        k3 = k2d.reshape(batch, seq, k2d.shape[1])

        o_fox = _attention(_fox_kernel, k3, t3, [kf], k_blk=0, q_blk=0,
                           v_blk=fox_blocks + sb_blocks, out_width=fox_w, name="fox_attn")
        o_sb = _attention(_sb_kernel, k3, t3, [], k_blk=fox_blocks, q_blk=fox_blocks,
                          v_blk=2 * fox_blocks + sb_blocks, out_width=sb_w, name="sb_attn")

        x2d = _out_proj(o_fox.reshape(batch * seq, fox_w), o_sb.reshape(batch * seq, sb_w), x2d,
                        fox_out_g[l][None, :], sb_out_g[l][None, :], w_out[l].astype(BF16))
        last = l == depth - 1
        assert last, "depth > 1 needs an un-normalised FFN output between layers"
        x2d = _conv_ffn(x2d, ffn_norm_g[l][None, :], w_up[l].astype(BF16), conv_w[l],
                        conv_b[l][None, :], w_down[l].astype(BF16), final_norm_g[None, :], seq=seq)
    return x2d.reshape(batch, seq, d)
```

```python
import functools

import jax
import jax.numpy as jnp
from jax import lax
from jax.experimental import pallas as pl
from jax.experimental.pallas import tpu as pltpu

HEAD_DIM = 64
EPS = 1e-6
CONV_WIDTH = 3

LANES = 128
HEADS_PER_STEP = LANES // HEAD_DIM
BF16_SUBLANES = 16
VMEM_LIMIT_BYTES = 56 * 1024 * 1024

ROW_TILE = 512
PROJ_CHUNK = 512
ATT_TILE = 512
SUFFIX_BLOCK = 256
CUMSUM_CHUNK = 256
FFN_CHUNK = 256
F_PIECES = 3
F_PIECE_STRIDE = 32

F32 = jnp.float32
BF16 = jnp.bfloat16
_NT = (((1,), (1,)), ((), ()))
LOG2_E = 1.4426950408889634
DEAD_COST_LOG2 = 160.0


def _compiler_params(semantics):
    return pltpu.CompilerParams(dimension_semantics=semantics,
                                vmem_limit_bytes=VMEM_LIMIT_BYTES)


def _rmsnorm_bf16(x, g):
    ms = jnp.mean(x * x, axis=-1, keepdims=True)
    return (x * lax.rsqrt(ms + EPS) * g).astype(BF16)


def _log_sigmoid(x):
    return jnp.minimum(x, 0.0) - jnp.log1p(jnp.exp(-jnp.abs(x)))


def _in_proj_kernel(x_ref, g_ref, wk_ref, wt_ref, wft_ref, k_ref, t_ref, ft_ref, *,
                    q_rows, scale):
    h = _rmsnorm_bf16(x_ref[...], g_ref[...])
    for c in range(wk_ref.shape[1] // PROJ_CHUNK):
        sl = slice(c * PROJ_CHUNK, (c + 1) * PROJ_CHUNK)
        k_ref[:, sl] = jnp.dot(h, wk_ref[:, sl], preferred_element_type=F32).astype(BF16)
    for c in range(wt_ref.shape[0] // PROJ_CHUNK):
        sl = slice(c * PROJ_CHUNK, (c + 1) * PROJ_CHUNK)
        y = lax.dot_general(wt_ref[sl, :], h, _NT, preferred_element_type=F32)
        if (c + 1) * PROJ_CHUNK <= q_rows:
            y = y * scale
        t_ref[0, sl, :] = y.astype(BF16)
    ft_ref[0] = lax.dot_general(wft_ref[...], h, _NT, preferred_element_type=F32)


def _in_proj(x2d, g, w_k, w_t, wft, *, batch, seq, q_rows, scale):
    n_rows, d = x2d.shape
    k_cols = w_k.shape[1]
    t_rows = w_t.shape[0]
    n_heads = wft.shape[0]
    tiles_per_seq = seq // ROW_TILE
    pos_block = lambda i: (i // tiles_per_seq, 0, i % tiles_per_seq)
    return pl.pallas_call(
        functools.partial(_in_proj_kernel, q_rows=q_rows, scale=scale),
        grid=(n_rows // ROW_TILE,),
        in_specs=[
            pl.BlockSpec((ROW_TILE, d), lambda i: (i, 0)),
            pl.BlockSpec((1, d), lambda i: (0, 0)),
            pl.BlockSpec((d, k_cols), lambda i: (0, 0)),
            pl.BlockSpec((t_rows, d), lambda i: (0, 0)),
            pl.BlockSpec((n_heads, d), lambda i: (0, 0)),
        ],
        out_specs=[
            pl.BlockSpec((ROW_TILE, k_cols), lambda i: (i, 0)),
            pl.BlockSpec((1, t_rows, ROW_TILE), pos_block),
            pl.BlockSpec((1, n_heads, ROW_TILE), pos_block),
        ],
        out_shape=[
            jax.ShapeDtypeStruct((n_rows, k_cols), BF16),
            jax.ShapeDtypeStruct((batch, t_rows, seq), BF16),
            jax.ShapeDtypeStruct((batch, n_heads, seq), F32),
        ],
        compiler_params=_compiler_params(("parallel",)),
        name="in_proj",
    )(x2d, g, w_k, w_t, wft)


def _forget_cumsum_kernel(ft_ref, b_ref, kf_ref, fend_ref):
    rows, seq = ft_ref.shape
    r = lax.broadcasted_iota(jnp.int32, (CUMSUM_CHUNK, CUMSUM_CHUNK), 0)
    c = lax.broadcasted_iota(jnp.int32, (CUMSUM_CHUNK, CUMSUM_CHUNK), 1)
    upper = (r <= c).astype(F32)
    pad = jnp.zeros((LANES - F_PIECES * F_PIECE_STRIDE, CUMSUM_CHUNK), F32)
    carry = jnp.zeros((rows, 1), F32)
    for ci in range(seq // CUMSUM_CHUNK):
        sl = slice(ci * CUMSUM_CHUNK, (ci + 1) * CUMSUM_CHUNK)
        log_f = _log_sigmoid(ft_ref[:, sl] + b_ref[...])
        local = jnp.dot(log_f, upper, precision=lax.Precision.HIGHEST,
                        preferred_element_type=F32)
        rest = (local + carry) * LOG2_E
        carry = carry + local[:, CUMSUM_CHUNK - 1:CUMSUM_CHUNK]
        chunk_end = (ci + 1) * CUMSUM_CHUNK
        if chunk_end % ATT_TILE == 0:
            blk = chunk_end // ATT_TILE - 1
            fend_ref[:, blk:blk + 1] = rest[:, CUMSUM_CHUNK - 1:CUMSUM_CHUNK]
        pieces = []
        for _ in range(F_PIECES):
            piece = rest.astype(BF16).astype(F32)
            pieces.append(piece)
            rest = rest - piece
        kf_ref[sl, :] = jnp.concatenate(pieces + [pad], axis=0).T.astype(BF16)


def _forget_cumsum(ft2d, bias_col):
    rows, seq = ft2d.shape
    assert rows == F_PIECE_STRIDE
    return pl.pallas_call(
        _forget_cumsum_kernel,
        grid=(1,),
        in_specs=[pl.BlockSpec((rows, seq), lambda i: (0, 0)),
                  pl.BlockSpec((rows, 1), lambda i: (0, 0))],
        out_specs=[pl.BlockSpec((seq, LANES), lambda i: (0, 0)),
                   pl.BlockSpec((rows, seq // ATT_TILE), lambda i: (0, 0))],
        out_shape=[jax.ShapeDtypeStruct((seq, LANES), BF16),
                   jax.ShapeDtypeStruct((rows, seq // ATT_TILE), F32)],
        compiler_params=_compiler_params(("arbitrary",)),
        name="forget_cumsum",
    )(ft2d, bias_col)


def _head_row_masks():
    sub = lax.broadcasted_iota(jnp.int32, (LANES, 1), 0)
    first = sub < HEAD_DIM
    return (first, jnp.logical_not(first))


def _key_block(ref, j):
    k0 = pl.multiple_of(j * ATT_TILE, ATT_TILE)
    return ref[0, pl.ds(k0, ATT_TILE), :]


def _value_block(ref, j):
    k0 = pl.multiple_of(j * ATT_TILE, ATT_TILE)
    return ref[0, :, pl.ds(k0, ATT_TILE)]


def _fox_kernel(qt_ref, k_ref, vt_ref, kf_ref, fend_ref, o_ref, s_scr, kmax_scr):
    b, p, i = pl.program_id(0), pl.program_id(1), pl.program_id(2)
    heads = range(HEADS_PER_STEP)
    row_masks = _head_row_masks()
    qt = qt_ref[0]
    sub = lax.broadcasted_iota(jnp.int32, (LANES, 1), 0)
    f_rows_used = sub < F_PIECES * F_PIECE_STRIDE
    f_row = [(b * pl.num_programs(1) + p) * HEADS_PER_STEP + h for h in heads]

    @pl.when(i == 0)
    def _():
        lane = lax.broadcasted_iota(jnp.int32, (1, LANES), 1)
        col_max = jnp.max(jnp.abs(k_ref[0].astype(F32)), axis=0, keepdims=True)
        for h in heads:
            kmax_scr[h] = jnp.max(jnp.where(lane // HEAD_DIM == h, col_max, 0.0))

    qpt, qk_bound = [], []
    for h in heads:
        pick = jnp.logical_and(sub % F_PIECE_STRIDE == f_row[h], f_rows_used)
        f_sel = jnp.broadcast_to(jnp.where(pick, -1.0, 0.0).astype(BF16), qt.shape)
        q_head = jnp.where(row_masks[h], qt, jnp.zeros_like(qt))
        qpt.append(jnp.concatenate([q_head, f_sel], axis=0))
        q_l1 = jnp.sum(jnp.abs(q_head.astype(F32)), axis=0, keepdims=True)
        qk_bound.append(kmax_scr[h] * q_l1)
    key = lax.broadcasted_iota(jnp.int32, (ATT_TILE, ATT_TILE), 0)
    qry = lax.broadcasted_iota(jnp.int32, (ATT_TILE, ATT_TILE), 1)
    causal = key <= qry

    def issue(j, slot):
        k0 = pl.multiple_of(j * ATT_TILE, ATT_TILE)
        kp = jnp.concatenate([_key_block(k_ref, j), kf_ref[pl.ds(k0, ATT_TILE), :]], axis=1)
        for h in heads:
            s_scr[slot, h] = jnp.dot(kp, qpt[h], preferred_element_type=F32)

    def consume(j, slot, state, masked):
        vt = _value_block(vt_ref, j)
        out = []
        for h in heads:
            m, acc = state[h]

            def tile():
                s = s_scr[slot, h]
                return jnp.where(causal, s, -jnp.inf) if masked else s

            v_aug = jnp.where(row_masks[h], vt, jnp.ones_like(vt))
            m_new = jnp.maximum(m, jnp.max(tile(), axis=0, keepdims=True))
            alpha = jnp.exp2(m - m_new)
            prob = jnp.exp2(tile() - m_new).astype(BF16)
            acc = alpha * acc + jnp.dot(v_aug, prob, preferred_element_type=F32)
            out.append((m_new, acc))
        return tuple(out)

    def still_live(st, next_j):
        live = False
        for h in heads:
            m, _ = st[h]
            headroom = jnp.max(qk_bound[h] - m) - fend_ref[f_row[h], next_j]
            live = jnp.logical_or(live, headroom >= -DEAD_COST_LOG2)
        return live

    issue(i, 0)
    issue(jnp.maximum(i - 1, 0), 1)
    state = tuple((jnp.full((1, ATT_TILE), -jnp.inf, F32), jnp.zeros((LANES, ATT_TILE), F32))
                  for _ in heads)
    state = consume(i, 0, state, True)

    def block_pair(carry):
        u, st, _ = carry
        j = i - 1 - 2 * u
        live_after = still_live(st, jnp.maximum(j - 2, 0))
        issue(j - 1, 0)
        st = consume(j, 1, st, False)
        issue(jnp.maximum(j - 2, 0), 1)
        return u + 1, consume(j - 1, 0, st, False), live_after

    n_pairs = i // 2
    _, state, live = lax.while_loop(
        lambda c: jnp.logical_and(c[0] < n_pairs, c[2]), block_pair,
        (jnp.int32(0), state, still_live(state, jnp.maximum(i - 1, 0))))
    state = lax.cond(jnp.logical_and(i % 2 == 1, live),
                     lambda st: consume(0, 1, st, False), lambda st: st, state)
    o = [acc / pltpu.roll(acc, HEAD_DIM, 0) for _, acc in state]
    o_ref[0] = jnp.where(row_masks[0], o[0], o[1]).T


def _sb_kernel(qt_ref, k_ref, vt_ref, o_ref, z_scr):
    i = pl.program_id(2)
    heads = range(HEADS_PER_STEP)
    row_masks = _head_row_masks()
    qt = qt_ref[0]
    qmt = [jnp.where(row_masks[h], qt, jnp.zeros_like(qt)) for h in heads]
    key = lax.broadcasted_iota(jnp.int32, (ATT_TILE, ATT_TILE), 0)
    qry = lax.broadcasted_iota(jnp.int32, (ATT_TILE, ATT_TILE), 1)
    strict = key < qry
    tr = lax.broadcasted_iota(jnp.int32, (SUFFIX_BLOCK, SUFFIX_BLOCK), 0)
    tc = lax.broadcasted_iota(jnp.int32, (SUFFIX_BLOCK, SUFFIX_BLOCK), 1)
    not_before = (tc >= tr).astype(BF16)
    n_sub = ATT_TILE // SUFFIX_BLOCK

    def issue(j, slot):
        kj = _key_block(k_ref, j)
        for h in heads:
            z_scr[slot, h] = jnp.dot(kj, qmt[h], preferred_element_type=F32)

    def log_weights(slot, h, masked):
        z = z_scr[slot, h]
        cost = jnp.maximum(z, 0.0) + jnp.log2(1.0 + jnp.exp2(-jnp.abs(z)))
        if masked:
            cost = jnp.where(strict, cost, 0.0)
        below = jnp.zeros((1, ATT_TILE), F32)
        parts = [None] * n_sub
        for u in reversed(range(n_sub)):
            x = cost[u * SUFFIX_BLOCK:(u + 1) * SUFFIX_BLOCK]
            hi = x.astype(BF16)
            lo = (x - hi.astype(F32)).astype(BF16)
            inclusive = (jnp.dot(not_before, hi, preferred_element_type=F32)
                         + jnp.dot(not_before, lo, preferred_element_type=F32))
            parts[u] = inclusive + below
            below = below + inclusive[0:1, :]
        return z_scr[slot, h] - jnp.concatenate(parts, axis=0), below

    def consume(j, slot, state, masked):
        vt = _value_block(vt_ref, j)
        logs = [log_weights(slot, h, masked) for h in heads]
        out = []
        for h in heads:
            r_cost, acc = state[h]
            log_w, block_cost = logs[h]
            w = jnp.exp2(log_w)
            if masked:
                w = jnp.where(strict, w, 0.0)
            pv = jnp.dot(vt, w.astype(BF16), preferred_element_type=F32)
            out.append((r_cost + block_cost, acc + jnp.exp2(-r_cost) * pv))
        return tuple(out)

    issue(i, 0)
    issue(jnp.maximum(i - 1, 0), 1)
    state = tuple((jnp.zeros((1, ATT_TILE), F32), jnp.zeros((LANES, ATT_TILE), F32))
                  for _ in heads)
    state = consume(i, 0, state, True)

    def still_live(st):
        lowest = functools.reduce(jnp.minimum, [jnp.min(r_cost) for r_cost, _ in st])
        return lowest < DEAD_COST_LOG2

    def block_pair(carry):
        u, st, _ = carry
        j = i - 1 - 2 * u
        issue(j - 1, 0)
        st = consume(j, 1, st, False)

        def second(st):
            issue(jnp.maximum(j - 2, 0), 1)
            return consume(j - 1, 0, st, False)

        st = lax.cond(still_live(st), second, lambda st: st, st)
        return u + 1, st, still_live(st)

    n_pairs = i // 2
    _, state, live = lax.while_loop(lambda c: jnp.logical_and(c[0] < n_pairs, c[2]),
                                    block_pair, (jnp.int32(0), state, still_live(state)))
    state = lax.cond(jnp.logical_and(i % 2 == 1, live),
                     lambda st: consume(0, 1, st, False), lambda st: st, state)
    o_ref[0] = jnp.where(row_masks[0], state[0][1], state[1][1]).T


def _attention(kernel, k3, t3, vmem_extra, smem_extra, extra_scratch, *, k_blk, q_blk, v_blk,
               out_width, name):
    batch, seq, _ = k3.shape
    n_pairs = out_width // LANES
    in_specs = [
        pl.BlockSpec((1, LANES, ATT_TILE), lambda b, p, i: (b, q_blk + p, i)),
        pl.BlockSpec((1, seq, LANES), lambda b, p, i: (b, 0, k_blk + p)),
        pl.BlockSpec((1, LANES, seq), lambda b, p, i: (b, v_blk + p, 0)),
    ]
    in_specs += [pl.BlockSpec(a.shape, lambda b, p, i: (0, 0)) for a in vmem_extra]
    in_specs += [pl.BlockSpec(memory_space=pltpu.SMEM) for _ in smem_extra]
    return pl.pallas_call(
        kernel,
        grid=(batch, n_pairs, seq // ATT_TILE),
        in_specs=in_specs,
        out_specs=pl.BlockSpec((1, ATT_TILE, LANES), lambda b, p, i: (b, i, p)),
        out_shape=jax.ShapeDtypeStruct((batch, seq, out_width), F32),
        scratch_shapes=[pltpu.VMEM((2, HEADS_PER_STEP, ATT_TILE, ATT_TILE), F32)] + extra_scratch,
        compiler_params=_compiler_params(("parallel", "parallel", "arbitrary")),
        name=name,
    )(t3, k3, t3, *vmem_extra, *smem_extra)


def _out_proj_kernel(of_ref, os_ref, x_ref, gf_ref, gs_ref, w_ref, o_ref):
    a = _rmsnorm_bf16(of_ref[...], gf_ref[...])
    b = _rmsnorm_bf16(os_ref[...], gs_ref[...])
    wf = a.shape[1]
    y = (jnp.dot(a, w_ref[:wf, :], preferred_element_type=F32)
         + jnp.dot(b, w_ref[wf:, :], preferred_element_type=F32))
    o_ref[...] = x_ref[...] + y


def _out_proj(o_fox, o_sb, x2d, g_fox, g_sb, w):
    n_rows, d = x2d.shape
    wf, ws = o_fox.shape[1], o_sb.shape[1]
    return pl.pallas_call(
        _out_proj_kernel,
        grid=(n_rows // ROW_TILE,),
        in_specs=[
            pl.BlockSpec((ROW_TILE, wf), lambda i: (i, 0)),
            pl.BlockSpec((ROW_TILE, ws), lambda i: (i, 0)),
            pl.BlockSpec((ROW_TILE, d), lambda i: (i, 0)),
            pl.BlockSpec((1, wf), lambda i: (0, 0)),
            pl.BlockSpec((1, ws), lambda i: (0, 0)),
            pl.BlockSpec((wf + ws, d), lambda i: (0, 0)),
        ],
        out_specs=pl.BlockSpec((ROW_TILE, d), lambda i: (i, 0)),
        out_shape=jax.ShapeDtypeStruct((n_rows, d), F32),
        compiler_params=_compiler_params(("parallel",)),
        name="out_proj",
    )(o_fox, o_sb, x2d, g_fox, g_sb, w)


def _conv_ffn_kernel(x_ref, halo_ref, g_ref, wup_ref, cw_ref, cb_ref, wdn_ref, gfin_ref,
                     o_ref, acc_ref, *, tiles_per_seq, d_ff):
    i = pl.program_id(0)
    x = x_ref[...]
    g = g_ref[...]
    h = _rmsnorm_bf16(x, g)
    h_halo = _rmsnorm_bf16(halo_ref[...], g)
    h_halo = jnp.where(i % tiles_per_seq == 0, jnp.zeros_like(h_halo), h_halo)
    h_ext = jnp.concatenate([h_halo, h], axis=0)

    def conv(u_ext, cols):
        out = cb_ref[:, cols] + cw_ref[CONV_WIDTH - 1:CONV_WIDTH, cols] * u_ext[BF16_SUBLANES:]
        for back in range(1, CONV_WIDTH):
            shifted = pltpu.roll(u_ext, back, 0)[BF16_SUBLANES:]
            tap = CONV_WIDTH - 1 - back
            out = out + cw_ref[tap:tap + 1, cols] * shifted
        return out

    def chunk_cols(c):
        return (slice(c * FFN_CHUNK, (c + 1) * FFN_CHUNK),
                slice(d_ff + c * FFN_CHUNK, d_ff + (c + 1) * FFN_CHUNK))

    def up_proj(c):
        return tuple(jnp.dot(h_ext, wup_ref[:, cols], preferred_element_type=F32)
                     for cols in chunk_cols(c))

    n_chunks = d_ff // FFN_CHUNK
    u_next = up_proj(0)
    for c in range(n_chunks):
        gate_cols, val_cols = chunk_cols(c)
        u_gate, u_val = u_next
        if c + 1 < n_chunks:
            u_next = up_proj(c + 1)
        gate = conv(u_gate, gate_cols)
        val = conv(u_val, val_cols)
        act = (gate / (1.0 + jnp.exp(-gate)) * val).astype(BF16)
        y = jnp.dot(act, wdn_ref[gate_cols, :], preferred_element_type=F32)
        if c == 0:
            acc_ref[...] = y
        else:
            acc_ref[...] += y

    x2 = x + acc_ref[...]
    ms = jnp.mean(x2 * x2, axis=-1, keepdims=True)
    o_ref[...] = x2 * lax.rsqrt(ms + EPS) * gfin_ref[...]


def _conv_ffn(x2d, g, w_up, conv_w, conv_b, w_down, g_final, *, seq):
    n_rows, d = x2d.shape
    d_ff = w_down.shape[0]
    tiles_per_seq = seq // ROW_TILE
    halo_blocks_per_tile = ROW_TILE // BF16_SUBLANES
    return pl.pallas_call(
        functools.partial(_conv_ffn_kernel, tiles_per_seq=tiles_per_seq, d_ff=d_ff),
        grid=(n_rows // ROW_TILE,),
        in_specs=[
            pl.BlockSpec((ROW_TILE, d), lambda i: (i, 0)),
            pl.BlockSpec((BF16_SUBLANES, d),
                         lambda i: (jnp.maximum(i * halo_blocks_per_tile - 1, 0), 0)),
            pl.BlockSpec((1, d), lambda i: (0, 0)),
            pl.BlockSpec((d, 2 * d_ff), lambda i: (0, 0)),
            pl.BlockSpec((CONV_WIDTH, 2 * d_ff), lambda i: (0, 0)),
            pl.BlockSpec((1, 2 * d_ff), lambda i: (0, 0)),
            pl.BlockSpec((d_ff, d), lambda i: (0, 0)),
            pl.BlockSpec((1, d), lambda i: (0, 0)),
        ],
        out_specs=pl.BlockSpec((ROW_TILE, d), lambda i: (i, 0)),
        out_shape=jax.ShapeDtypeStruct((n_rows, d), F32),
        scratch_shapes=[pltpu.VMEM((ROW_TILE, d), F32)],
        compiler_params=_compiler_params(("parallel",)),
        name="conv_ffn",
    )(x2d, x2d, g, w_up, conv_w, conv_b, w_down, g_final)


def kernel(x, attn_norm_g, w_in, forget_bias, fox_out_g, sb_out_g, w_out, ffn_norm_g, w_up,
           conv_w, conv_b, w_down, final_norm_g):
    batch, seq, d = x.shape
    depth = w_in.shape[0]
    n_fox = forget_bias.shape[1]
    fox_w = fox_out_g.shape[1]
    sb_w = sb_out_g.shape[1]
    assert seq % ROW_TILE == 0 and seq % ATT_TILE == 0 and ATT_TILE % SUFFIX_BLOCK == 0
    assert fox_w == n_fox * HEAD_DIM and n_fox % HEADS_PER_STEP == 0
    assert fox_w % PROJ_CHUNK == 0 and sb_w % PROJ_CHUNK == 0
    assert w_down.shape[1] % FFN_CHUNK == 0

    scale = LOG2_E * HEAD_DIM ** -0.5
    fox_blocks = fox_w // LANES
    sb_blocks = sb_w // LANES
    c_fq, c_fk, c_fv, c_fl = 0, fox_w, 2 * fox_w, 3 * fox_w
    c_sq = c_fl + n_fox
    c_sk, c_sv = c_sq + sb_w, c_sq + 2 * sb_w

    x2d = x.reshape(batch * seq, d)
    for l in range(depth):
        w_l = w_in[l]
        cols = lambda start, width: w_l[:, start:start + width]
        w_k = jnp.concatenate([cols(c_fk, fox_w), cols(c_sk, sb_w)], axis=1).astype(BF16)
        w_t = jnp.concatenate([cols(c_fq, fox_w), cols(c_sq, sb_w),
                               cols(c_fv, fox_w), cols(c_sv, sb_w)], axis=1).T.astype(BF16)
        wft = cols(c_fl, n_fox).T.astype(BF16)
        k2d, t3, f_logit_t = _in_proj(x2d, attn_norm_g[l][None, :], w_k, w_t, wft,
                                      batch=batch, seq=seq, q_rows=fox_w + sb_w, scale=scale)
        bias_col = jnp.tile(forget_bias[l], batch)[:, None]
        kf, f_end = _forget_cumsum(f_logit_t.reshape(batch * n_fox, seq), bias_col)
        k3 = k2d.reshape(batch, seq, k2d.shape[1])

        o_fox = _attention(_fox_kernel, k3, t3, [kf], [f_end],
                           [pltpu.SMEM((HEADS_PER_STEP,), F32)], k_blk=0, q_blk=0,
                           v_blk=fox_blocks + sb_blocks, out_width=fox_w, name="fox_attn")
        o_sb = _attention(_sb_kernel, k3, t3, [], [], [], k_blk=fox_blocks, q_blk=fox_blocks,
                          v_blk=2 * fox_blocks + sb_blocks, out_width=sb_w, name="sb_attn")

        x2d = _out_proj(o_fox.reshape(batch * seq, fox_w), o_sb.reshape(batch * seq, sb_w), x2d,
                        fox_out_g[l][None, :], sb_out_g[l][None, :], w_out[l].astype(BF16))
        last = l == depth - 1
        assert last, "depth > 1 needs an un-normalised FFN output between layers"
        x2d = _conv_ffn(x2d, ffn_norm_g[l][None, :], w_up[l].astype(BF16), conv_w[l],
                        conv_b[l][None, :], w_down[l].astype(BF16), final_norm_g[None, :], seq=seq)
    return x2d.reshape(batch, seq, d)
```

```python
import functools

import jax
import jax.numpy as jnp
from jax import lax
from jax.experimental import pallas as pl
from jax.experimental.pallas import tpu as pltpu

HEAD_DIM = 64
EPS = 1e-6
CONV_WIDTH = 3

LANES = 128
HEADS_PER_STEP = LANES // HEAD_DIM
BF16_SUBLANES = 16
VMEM_LIMIT_BYTES = 56 * 1024 * 1024

ROW_TILE = 512
PROJ_CHUNK = 512
FOX_TILE = 512
SB_TILE = 256
SUFFIX_BLOCK = 256
CUMSUM_CHUNK = 256
FFN_CHUNK = 256
F_PIECES = 3
F_PIECE_STRIDE = 32

F32 = jnp.float32
BF16 = jnp.bfloat16
_NT = (((1,), (1,)), ((), ()))
LOG2_E = 1.4426950408889634
DEAD_COST_LOG2 = 160.0


def _compiler_params(semantics):
    return pltpu.CompilerParams(dimension_semantics=semantics,
                                vmem_limit_bytes=VMEM_LIMIT_BYTES)


def _rmsnorm_bf16(x, g):
    ms = jnp.mean(x * x, axis=-1, keepdims=True)
    return (x * lax.rsqrt(ms + EPS) * g).astype(BF16)


def _log_sigmoid(x):
    return jnp.minimum(x, 0.0) - jnp.log1p(jnp.exp(-jnp.abs(x)))


def _in_proj_kernel(x_ref, g_ref, wk_ref, wt_ref, wft_ref, k_ref, t_ref, ft_ref, *,
                    q_rows, scale):
    h = _rmsnorm_bf16(x_ref[...], g_ref[...])
    for c in range(wk_ref.shape[1] // PROJ_CHUNK):
        sl = slice(c * PROJ_CHUNK, (c + 1) * PROJ_CHUNK)
        k_ref[:, sl] = jnp.dot(h, wk_ref[:, sl], preferred_element_type=F32).astype(BF16)
    for c in range(wt_ref.shape[0] // PROJ_CHUNK):
        sl = slice(c * PROJ_CHUNK, (c + 1) * PROJ_CHUNK)
        y = lax.dot_general(wt_ref[sl, :], h, _NT, preferred_element_type=F32)
        if (c + 1) * PROJ_CHUNK <= q_rows:
            y = y * scale
        t_ref[0, sl, :] = y.astype(BF16)
    ft_ref[0] = lax.dot_general(wft_ref[...], h, _NT, preferred_element_type=F32)


def _in_proj(x2d, g, w_k, w_t, wft, *, batch, seq, q_rows, scale):
    n_rows, d = x2d.shape
    k_cols = w_k.shape[1]
    t_rows = w_t.shape[0]
    n_heads = wft.shape[0]
    tiles_per_seq = seq // ROW_TILE
    pos_block = lambda i: (i // tiles_per_seq, 0, i % tiles_per_seq)
    return pl.pallas_call(
        functools.partial(_in_proj_kernel, q_rows=q_rows, scale=scale),
        grid=(n_rows // ROW_TILE,),
        in_specs=[
            pl.BlockSpec((ROW_TILE, d), lambda i: (i, 0)),
            pl.BlockSpec((1, d), lambda i: (0, 0)),
            pl.BlockSpec((d, k_cols), lambda i: (0, 0)),
            pl.BlockSpec((t_rows, d), lambda i: (0, 0)),
            pl.BlockSpec((n_heads, d), lambda i: (0, 0)),
        ],
        out_specs=[
            pl.BlockSpec((ROW_TILE, k_cols), lambda i: (i, 0)),
            pl.BlockSpec((1, t_rows, ROW_TILE), pos_block),
            pl.BlockSpec((1, n_heads, ROW_TILE), pos_block),
        ],
        out_shape=[
            jax.ShapeDtypeStruct((n_rows, k_cols), BF16),
            jax.ShapeDtypeStruct((batch, t_rows, seq), BF16),
            jax.ShapeDtypeStruct((batch, n_heads, seq), F32),
        ],
        compiler_params=_compiler_params(("parallel",)),
        name="in_proj",
    )(x2d, g, w_k, w_t, wft)


def _forget_cumsum_kernel(ft_ref, b_ref, kf_ref, fend_ref):
    rows, seq = ft_ref.shape
    r = lax.broadcasted_iota(jnp.int32, (CUMSUM_CHUNK, CUMSUM_CHUNK), 0)
    c = lax.broadcasted_iota(jnp.int32, (CUMSUM_CHUNK, CUMSUM_CHUNK), 1)
    upper = (r <= c).astype(F32)
    pad = jnp.zeros((LANES - F_PIECES * F_PIECE_STRIDE, CUMSUM_CHUNK), F32)
    carry = jnp.zeros((rows, 1), F32)
    for ci in range(seq // CUMSUM_CHUNK):
        sl = slice(ci * CUMSUM_CHUNK, (ci + 1) * CUMSUM_CHUNK)
        log_f = _log_sigmoid(ft_ref[:, sl] + b_ref[...])
        local = jnp.dot(log_f, upper, precision=lax.Precision.HIGHEST,
                        preferred_element_type=F32)
        rest = (local + carry) * LOG2_E
        carry = carry + local[:, CUMSUM_CHUNK - 1:CUMSUM_CHUNK]
        chunk_end = (ci + 1) * CUMSUM_CHUNK
        if chunk_end % FOX_TILE == 0:
            blk = chunk_end // FOX_TILE - 1
            fend_ref[:, blk:blk + 1] = rest[:, CUMSUM_CHUNK - 1:CUMSUM_CHUNK]
        pieces = []
        for _ in range(F_PIECES):
            piece = rest.astype(BF16).astype(F32)
            pieces.append(piece)
            rest = rest - piece
        kf_ref[sl, :] = jnp.concatenate(pieces + [pad], axis=0).T.astype(BF16)


def _forget_cumsum(ft2d, bias_col):
    rows, seq = ft2d.shape
    assert rows == F_PIECE_STRIDE
    return pl.pallas_call(
        _forget_cumsum_kernel,
        grid=(1,),
        in_specs=[pl.BlockSpec((rows, seq), lambda i: (0, 0)),
                  pl.BlockSpec((rows, 1), lambda i: (0, 0))],
        out_specs=[pl.BlockSpec((seq, LANES), lambda i: (0, 0)),
                   pl.BlockSpec((rows, seq // FOX_TILE), lambda i: (0, 0))],
        out_shape=[jax.ShapeDtypeStruct((seq, LANES), BF16),
                   jax.ShapeDtypeStruct((rows, seq // FOX_TILE), F32)],
        compiler_params=_compiler_params(("arbitrary",)),
        name="forget_cumsum",
    )(ft2d, bias_col)


def _head_row_masks():
    sub = lax.broadcasted_iota(jnp.int32, (LANES, 1), 0)
    first = sub < HEAD_DIM
    return (first, jnp.logical_not(first))


def _key_block(ref, j, tile):
    k0 = pl.multiple_of(j * tile, tile)
    return ref[0, pl.ds(k0, tile), :]


def _value_block(ref, j, tile):
    k0 = pl.multiple_of(j * tile, tile)
    return ref[0, :, pl.ds(k0, tile)]


def _fox_kernel(qt_ref, k_ref, vt_ref, kf_ref, fend_ref, o_ref, s_scr, kmax_scr):
    b, p, i = pl.program_id(0), pl.program_id(1), pl.program_id(2)
    tile = qt_ref.shape[2]
    heads = range(HEADS_PER_STEP)
    row_masks = _head_row_masks()
    qt = qt_ref[0]
    sub = lax.broadcasted_iota(jnp.int32, (LANES, 1), 0)
    f_rows_used = sub < F_PIECES * F_PIECE_STRIDE
    f_row = [(b * pl.num_programs(1) + p) * HEADS_PER_STEP + h for h in heads]

    @pl.when(i == 0)
    def _():
        lane = lax.broadcasted_iota(jnp.int32, (1, LANES), 1)
        col_max = jnp.max(jnp.abs(k_ref[0].astype(F32)), axis=0, keepdims=True)
        for h in heads:
            kmax_scr[h] = jnp.max(jnp.where(lane // HEAD_DIM == h, col_max, 0.0))

    qpt, qk_bound = [], []
    for h in heads:
        pick = jnp.logical_and(sub % F_PIECE_STRIDE == f_row[h], f_rows_used)
        f_sel = jnp.broadcast_to(jnp.where(pick, -1.0, 0.0).astype(BF16), qt.shape)
        q_head = jnp.where(row_masks[h], qt, jnp.zeros_like(qt))
        qpt.append(jnp.concatenate([q_head, f_sel], axis=0))
        q_l1 = jnp.sum(jnp.abs(q_head.astype(F32)), axis=0, keepdims=True)
        qk_bound.append(kmax_scr[h] * q_l1)
    key = lax.broadcasted_iota(jnp.int32, (tile, tile), 0)
    qry = lax.broadcasted_iota(jnp.int32, (tile, tile), 1)
    causal = key <= qry

    def issue(j, slot):
        k0 = pl.multiple_of(j * tile, tile)
        kp = jnp.concatenate([_key_block(k_ref, j, tile), kf_ref[pl.ds(k0, tile), :]], axis=1)
        for h in heads:
            s_scr[slot, h] = jnp.dot(kp, qpt[h], preferred_element_type=F32)

    def consume(j, slot, state, masked):
        vt = _value_block(vt_ref, j, tile)
        out = []
        for h in heads:
            m, acc = state[h]

            def scores():
                s = s_scr[slot, h]
                return jnp.where(causal, s, -jnp.inf) if masked else s

            v_aug = jnp.where(row_masks[h], vt, jnp.ones_like(vt))
            m_new = jnp.maximum(m, jnp.max(scores(), axis=0, keepdims=True))
            alpha = jnp.exp2(m - m_new)
            prob = jnp.exp2(scores() - m_new).astype(BF16)
            acc = alpha * acc + jnp.dot(v_aug, prob, preferred_element_type=F32)
            out.append((m_new, acc))
        return tuple(out)

    def still_live(st, next_j):
        live = False
        for h in heads:
            m, _ = st[h]
            headroom = jnp.max(qk_bound[h] - m) - fend_ref[f_row[h], next_j]
            live = jnp.logical_or(live, headroom >= -DEAD_COST_LOG2)
        return live

    issue(i, 0)
    issue(jnp.maximum(i - 1, 0), 1)
    state = tuple((jnp.full((1, tile), -jnp.inf, F32), jnp.zeros((LANES, tile), F32))
                  for _ in heads)
    state = consume(i, 0, state, True)

    def block_pair(carry):
        u, st, _ = carry
        j = i - 1 - 2 * u
        live_after = still_live(st, jnp.maximum(j - 2, 0))
        issue(j - 1, 0)
        st = consume(j, 1, st, False)
        issue(jnp.maximum(j - 2, 0), 1)
        return u + 1, consume(j - 1, 0, st, False), live_after

    n_pairs = i // 2
    _, state, live = lax.while_loop(
        lambda c: jnp.logical_and(c[0] < n_pairs, c[2]), block_pair,
        (jnp.int32(0), state, still_live(state, jnp.maximum(i - 1, 0))))
    state = lax.cond(jnp.logical_and(i % 2 == 1, live),
                     lambda st: consume(0, 1, st, False), lambda st: st, state)
    o = [acc / pltpu.roll(acc, HEAD_DIM, 0) for _, acc in state]
    o_ref[0] = jnp.where(row_masks[0], o[0], o[1]).T


def _sb_kernel(qt_ref, k_ref, vt_ref, o_ref, z_scr):
    i = pl.program_id(2)
    tile = qt_ref.shape[2]
    heads = range(HEADS_PER_STEP)
    row_masks = _head_row_masks()
    qt = qt_ref[0]
    qmt = [jnp.where(row_masks[h], qt, jnp.zeros_like(qt)) for h in heads]
    key = lax.broadcasted_iota(jnp.int32, (tile, tile), 0)
    qry = lax.broadcasted_iota(jnp.int32, (tile, tile), 1)
    strict = key < qry
    tr = lax.broadcasted_iota(jnp.int32, (SUFFIX_BLOCK, SUFFIX_BLOCK), 0)
    tc = lax.broadcasted_iota(jnp.int32, (SUFFIX_BLOCK, SUFFIX_BLOCK), 1)
    not_before = (tc >= tr).astype(BF16)
    n_sub = tile // SUFFIX_BLOCK

    def issue(j, slot):
        kj = _key_block(k_ref, j, tile)
        for h in heads:
            z_scr[slot, h] = jnp.dot(kj, qmt[h], preferred_element_type=F32)

    def log_weights(slot, h, masked):
        z = z_scr[slot, h]
        cost = jnp.maximum(z, 0.0) + jnp.log2(1.0 + jnp.exp2(-jnp.abs(z)))
        if masked:
            cost = jnp.where(strict, cost, 0.0)
        below = jnp.zeros((1, tile), F32)
        parts = [None] * n_sub
        for u in reversed(range(n_sub)):
            x = cost[u * SUFFIX_BLOCK:(u + 1) * SUFFIX_BLOCK]
            hi = x.astype(BF16)
            lo = (x - hi.astype(F32)).astype(BF16)
            inclusive = (jnp.dot(not_before, hi, preferred_element_type=F32)
                         + jnp.dot(not_before, lo, preferred_element_type=F32))
            parts[u] = inclusive + below
            below = below + inclusive[0:1, :]
        return z_scr[slot, h] - jnp.concatenate(parts, axis=0), below

    def accumulate(j, logs, state, masked, valid=None):
        vt = _value_block(vt_ref, j, tile)
        out = []
        for h in heads:
            r_cost, acc = state[h]
            log_w, block_cost = logs[h]
            w = jnp.exp2(log_w)
            if masked:
                w = jnp.where(strict, w, 0.0)
            pv = jnp.exp2(-r_cost) * jnp.dot(vt, w.astype(BF16), preferred_element_type=F32)
            if valid is not None:
                pv = jnp.where(valid, pv, 0.0)
                block_cost = jnp.where(valid, block_cost, 0.0)
            out.append((r_cost + block_cost, acc + pv))
        return tuple(out)

    def consume(j, slot, state):
        return accumulate(j, [log_weights(slot, h, False) for h in heads], state, False)

    def still_live(st):
        lowest = functools.reduce(jnp.minimum, [jnp.min(r_cost) for r_cost, _ in st])
        return lowest < DEAD_COST_LOG2

    prev = jnp.maximum(i - 1, 0)
    issue(i, 0)
    issue(prev, 1)
    state = tuple((jnp.zeros((1, tile), F32), jnp.zeros((LANES, tile), F32))
                  for _ in heads)
    logs_diag = [log_weights(0, h, True) for h in heads]
    logs_prev = [log_weights(1, h, False) for h in heads]
    state = accumulate(i, logs_diag, state, True)
    state = accumulate(prev, logs_prev, state, False, valid=i >= 1)
    issue(jnp.maximum(i - 2, 0), 0)

    def block_pair(carry):
        u, st, _ = carry
        j = i - 2 - 2 * u
        issue(j - 1, 1)
        st = consume(j, 0, st)

        def second(st):
            issue(jnp.maximum(j - 2, 0), 0)
            return consume(j - 1, 1, st)

        st = lax.cond(still_live(st), second, lambda st: st, st)
        return u + 1, st, still_live(st)

    n_left = jnp.maximum(i - 1, 0)
    _, state, live = lax.while_loop(lambda c: jnp.logical_and(c[0] < n_left // 2, c[2]),
                                    block_pair, (jnp.int32(0), state, still_live(state)))
    state = lax.cond(jnp.logical_and(n_left % 2 == 1, live),
                     lambda st: consume(0, 0, st), lambda st: st, state)
    o_ref[0] = jnp.where(row_masks[0], state[0][1], state[1][1]).T


def _attention(kernel, k3, t3, vmem_extra, smem_extra, extra_scratch, *, tile, k_blk, q_blk,
               v_blk, out_width, name):
    batch, seq, _ = k3.shape
    n_pairs = out_width // LANES
    in_specs = [
        pl.BlockSpec((1, LANES, tile), lambda b, p, i: (b, q_blk + p, i)),
        pl.BlockSpec((1, seq, LANES), lambda b, p, i: (b, 0, k_blk + p)),
        pl.BlockSpec((1, LANES, seq), lambda b, p, i: (b, v_blk + p, 0)),
    ]
    in_specs += [pl.BlockSpec(a.shape, lambda b, p, i: (0, 0)) for a in vmem_extra]
    in_specs += [pl.BlockSpec(memory_space=pltpu.SMEM) for _ in smem_extra]
    return pl.pallas_call(
        kernel,
        grid=(batch, n_pairs, seq // tile),
        in_specs=in_specs,
        out_specs=pl.BlockSpec((1, tile, LANES), lambda b, p, i: (b, i, p)),
        out_shape=jax.ShapeDtypeStruct((batch, seq, out_width), F32),
        scratch_shapes=[pltpu.VMEM((2, HEADS_PER_STEP, tile, tile), F32)] + extra_scratch,
        compiler_params=_compiler_params(("parallel", "parallel", "arbitrary")),
        name=name,
    )(t3, k3, t3, *vmem_extra, *smem_extra)


def _out_proj_kernel(of_ref, os_ref, x_ref, gf_ref, gs_ref, w_ref, o_ref):
    a = _rmsnorm_bf16(of_ref[...], gf_ref[...])
    b = _rmsnorm_bf16(os_ref[...], gs_ref[...])
    wf = a.shape[1]
    y = (jnp.dot(a, w_ref[:wf, :], preferred_element_type=F32)
         + jnp.dot(b, w_ref[wf:, :], preferred_element_type=F32))
    o_ref[...] = x_ref[...] + y


def _out_proj(o_fox, o_sb, x2d, g_fox, g_sb, w):
    n_rows, d = x2d.shape
    wf, ws = o_fox.shape[1], o_sb.shape[1]
    return pl.pallas_call(
        _out_proj_kernel,
        grid=(n_rows // ROW_TILE,),
        in_specs=[
            pl.BlockSpec((ROW_TILE, wf), lambda i: (i, 0)),
            pl.BlockSpec((ROW_TILE, ws), lambda i: (i, 0)),
            pl.BlockSpec((ROW_TILE, d), lambda i: (i, 0)),
            pl.BlockSpec((1, wf), lambda i: (0, 0)),
            pl.BlockSpec((1, ws), lambda i: (0, 0)),
            pl.BlockSpec((wf + ws, d), lambda i: (0, 0)),
        ],
        out_specs=pl.BlockSpec((ROW_TILE, d), lambda i: (i, 0)),
        out_shape=jax.ShapeDtypeStruct((n_rows, d), F32),
        compiler_params=_compiler_params(("parallel",)),
        name="out_proj",
    )(o_fox, o_sb, x2d, g_fox, g_sb, w)


def _conv_ffn_kernel(x_ref, halo_ref, g_ref, wup_ref, cw_ref, cb_ref, wdn_ref, gfin_ref,
                     o_ref, act_ref, *, tiles_per_seq, d_ff):
    i = pl.program_id(0)
    x = x_ref[...]
    g = g_ref[...]
    h = _rmsnorm_bf16(x, g)
    h_halo = _rmsnorm_bf16(halo_ref[...], g)
    h_halo = jnp.where(i % tiles_per_seq == 0, jnp.zeros_like(h_halo), h_halo)
    h_ext = jnp.concatenate([h_halo, h], axis=0)

    def conv(u_ext, cols):
        out = cb_ref[:, cols] + cw_ref[CONV_WIDTH - 1:CONV_WIDTH, cols] * u_ext[BF16_SUBLANES:]
        for back in range(1, CONV_WIDTH):
            shifted = pltpu.roll(u_ext, back, 0)[BF16_SUBLANES:]
            tap = CONV_WIDTH - 1 - back
            out = out + cw_ref[tap:tap + 1, cols] * shifted
        return out

    def chunk_cols(c):
        return (slice(c * FFN_CHUNK, (c + 1) * FFN_CHUNK),
                slice(d_ff + c * FFN_CHUNK, d_ff + (c + 1) * FFN_CHUNK))

    def up_proj(c):
        return tuple(jnp.dot(h_ext, wup_ref[:, cols], preferred_element_type=F32)
                     for cols in chunk_cols(c))

    n_chunks = d_ff // FFN_CHUNK
    u_next = up_proj(0)
    for c in range(n_chunks):
        gate_cols, val_cols = chunk_cols(c)
        u_gate, u_val = u_next
        if c + 1 < n_chunks:
            u_next = up_proj(c + 1)
        gate = conv(u_gate, gate_cols)
        val = conv(u_val, val_cols)
        act_ref[:, gate_cols] = (gate / (1.0 + jnp.exp(-gate)) * val).astype(BF16)

    x2 = x + jnp.dot(act_ref[...], wdn_ref[...], preferred_element_type=F32)
    ms = jnp.mean(x2 * x2, axis=-1, keepdims=True)
    o_ref[...] = x2 * lax.rsqrt(ms + EPS) * gfin_ref[...]


def _conv_ffn(x2d, g, w_up, conv_w, conv_b, w_down, g_final, *, seq):
    n_rows, d = x2d.shape
    d_ff = w_down.shape[0]
    tiles_per_seq = seq // ROW_TILE
    halo_blocks_per_tile = ROW_TILE // BF16_SUBLANES
    return pl.pallas_call(
        functools.partial(_conv_ffn_kernel, tiles_per_seq=tiles_per_seq, d_ff=d_ff),
        grid=(n_rows // ROW_TILE,),
        in_specs=[
            pl.BlockSpec((ROW_TILE, d), lambda i: (i, 0)),
            pl.BlockSpec((BF16_SUBLANES, d),
                         lambda i: (jnp.maximum(i * halo_blocks_per_tile - 1, 0), 0)),
            pl.BlockSpec((1, d), lambda i: (0, 0)),
            pl.BlockSpec((d, 2 * d_ff), lambda i: (0, 0)),
            pl.BlockSpec((CONV_WIDTH, 2 * d_ff), lambda i: (0, 0)),
            pl.BlockSpec((1, 2 * d_ff), lambda i: (0, 0)),
            pl.BlockSpec((d_ff, d), lambda i: (0, 0)),
            pl.BlockSpec((1, d), lambda i: (0, 0)),
        ],
        out_specs=pl.BlockSpec((ROW_TILE, d), lambda i: (i, 0)),
        out_shape=jax.ShapeDtypeStruct((n_rows, d), F32),
        scratch_shapes=[pltpu.VMEM((ROW_TILE, d_ff), BF16)],
        compiler_params=_compiler_params(("parallel",)),
        name="conv_ffn",
    )(x2d, x2d, g, w_up, conv_w, conv_b, w_down, g_final)


def kernel(x, attn_norm_g, w_in, forget_bias, fox_out_g, sb_out_g, w_out, ffn_norm_g, w_up,
           conv_w, conv_b, w_down, final_norm_g):
    batch, seq, d = x.shape
    depth = w_in.shape[0]
    n_fox = forget_bias.shape[1]
    fox_w = fox_out_g.shape[1]
    sb_w = sb_out_g.shape[1]
    assert seq % ROW_TILE == 0 and seq % FOX_TILE == 0 and seq % SB_TILE == 0
    assert SB_TILE % SUFFIX_BLOCK == 0
    assert fox_w == n_fox * HEAD_DIM and n_fox % HEADS_PER_STEP == 0
    assert fox_w % PROJ_CHUNK == 0 and sb_w % PROJ_CHUNK == 0
    assert w_down.shape[1] % FFN_CHUNK == 0

    scale = LOG2_E * HEAD_DIM ** -0.5
    fox_blocks = fox_w // LANES
    sb_blocks = sb_w // LANES
    c_fq, c_fk, c_fv, c_fl = 0, fox_w, 2 * fox_w, 3 * fox_w
    c_sq = c_fl + n_fox
    c_sk, c_sv = c_sq + sb_w, c_sq + 2 * sb_w

    x2d = x.reshape(batch * seq, d)
    for l in range(depth):
        w_l = w_in[l]
        cols = lambda start, width: w_l[:, start:start + width]
        w_k = jnp.concatenate([cols(c_fk, fox_w), cols(c_sk, sb_w)], axis=1).astype(BF16)
        w_t = jnp.concatenate([cols(c_fq, fox_w), cols(c_sq, sb_w),
                               cols(c_fv, fox_w), cols(c_sv, sb_w)], axis=1).T.astype(BF16)
        wft = cols(c_fl, n_fox).T.astype(BF16)
        k2d, t3, f_logit_t = _in_proj(x2d, attn_norm_g[l][None, :], w_k, w_t, wft,
                                      batch=batch, seq=seq, q_rows=fox_w + sb_w, scale=scale)
        bias_col = jnp.tile(forget_bias[l], batch)[:, None]
        kf, f_end = _forget_cumsum(f_logit_t.reshape(batch * n_fox, seq), bias_col)
        k3 = k2d.reshape(batch, seq, k2d.shape[1])

        o_fox = _attention(_fox_kernel, k3, t3, [kf], [f_end],
                           [pltpu.SMEM((HEADS_PER_STEP,), F32)], tile=FOX_TILE, k_blk=0, q_blk=0,
                           v_blk=fox_blocks + sb_blocks, out_width=fox_w, name="fox_attn")
        o_sb = _attention(_sb_kernel, k3, t3, [], [], [], tile=SB_TILE, k_blk=fox_blocks,
                          q_blk=fox_blocks, v_blk=2 * fox_blocks + sb_blocks, out_width=sb_w,
                          name="sb_attn")

        x2d = _out_proj(o_fox.reshape(batch * seq, fox_w), o_sb.reshape(batch * seq, sb_w), x2d,
                        fox_out_g[l][None, :], sb_out_g[l][None, :], w_out[l].astype(BF16))
        last = l == depth - 1
        assert last, "depth > 1 needs an un-normalised FFN output between layers"
        x2d = _conv_ffn(x2d, ffn_norm_g[l][None, :], w_up[l].astype(BF16), conv_w[l],
                        conv_b[l][None, :], w_down[l].astype(BF16), final_norm_g[None, :], seq=seq)
    return x2d.reshape(batch, seq, d)
```

```python
import functools

import jax
import jax.numpy as jnp
from jax import lax
from jax.experimental import pallas as pl
from jax.experimental.pallas import tpu as pltpu

HEAD_DIM = 64
EPS = 1e-6
CONV_WIDTH = 3

LANES = 128
HEADS_PER_STEP = LANES // HEAD_DIM
BF16_SUBLANES = 16
VMEM_LIMIT_BYTES = 56 * 1024 * 1024

ROW_TILE = 512
PROJ_CHUNK = 512
FOX_TILE = 512
SB_TILE = 256
SB_PAIRS_PER_STEP = 2
SUFFIX_BLOCK = 256
CUMSUM_CHUNK = 256
FFN_CHUNK = 256
F_PIECES = 3
F_PIECE_STRIDE = 32

F32 = jnp.float32
BF16 = jnp.bfloat16
_NT = (((1,), (1,)), ((), ()))
LOG2_E = 1.4426950408889634
DEAD_COST_LOG2 = 160.0


def _compiler_params(semantics):
    return pltpu.CompilerParams(dimension_semantics=semantics,
                                vmem_limit_bytes=VMEM_LIMIT_BYTES)


def _rmsnorm_bf16(x, g):
    ms = jnp.mean(x * x, axis=-1, keepdims=True)
    return (x * lax.rsqrt(ms + EPS) * g).astype(BF16)


def _log_sigmoid(x):
    return jnp.minimum(x, 0.0) - jnp.log1p(jnp.exp(-jnp.abs(x)))


def _in_proj_kernel(x_ref, g_ref, wk_ref, wt_ref, wft_ref, k_ref, t_ref, ft_ref, *,
                    q_rows, scale):
    h = _rmsnorm_bf16(x_ref[...], g_ref[...])
    for c in range(wk_ref.shape[1] // PROJ_CHUNK):
        sl = slice(c * PROJ_CHUNK, (c + 1) * PROJ_CHUNK)
        k_ref[:, sl] = jnp.dot(h, wk_ref[:, sl], preferred_element_type=F32).astype(BF16)
    for c in range(wt_ref.shape[0] // PROJ_CHUNK):
        sl = slice(c * PROJ_CHUNK, (c + 1) * PROJ_CHUNK)
        y = lax.dot_general(wt_ref[sl, :], h, _NT, preferred_element_type=F32)
        if (c + 1) * PROJ_CHUNK <= q_rows:
            y = y * scale
        t_ref[0, sl, :] = y.astype(BF16)
    ft_ref[0] = lax.dot_general(wft_ref[...], h, _NT, preferred_element_type=F32)


def _in_proj(x2d, g, w_k, w_t, wft, *, batch, seq, q_rows, scale):
    n_rows, d = x2d.shape
    k_cols = w_k.shape[1]
    t_rows = w_t.shape[0]
    n_heads = wft.shape[0]
    tiles_per_seq = seq // ROW_TILE
    pos_block = lambda i: (i // tiles_per_seq, 0, i % tiles_per_seq)
    return pl.pallas_call(
        functools.partial(_in_proj_kernel, q_rows=q_rows, scale=scale),
        grid=(n_rows // ROW_TILE,),
        in_specs=[
            pl.BlockSpec((ROW_TILE, d), lambda i: (i, 0)),
            pl.BlockSpec((1, d), lambda i: (0, 0)),
            pl.BlockSpec((d, k_cols), lambda i: (0, 0)),
            pl.BlockSpec((t_rows, d), lambda i: (0, 0)),
            pl.BlockSpec((n_heads, d), lambda i: (0, 0)),
        ],
        out_specs=[
            pl.BlockSpec((ROW_TILE, k_cols), lambda i: (i, 0)),
            pl.BlockSpec((1, t_rows, ROW_TILE), pos_block),
            pl.BlockSpec((1, n_heads, ROW_TILE), pos_block),
        ],
        out_shape=[
            jax.ShapeDtypeStruct((n_rows, k_cols), BF16),
            jax.ShapeDtypeStruct((batch, t_rows, seq), BF16),
            jax.ShapeDtypeStruct((batch, n_heads, seq), F32),
        ],
        compiler_params=_compiler_params(("parallel",)),
        name="in_proj",
    )(x2d, g, w_k, w_t, wft)


def _forget_cumsum_kernel(ft_ref, b_ref, kf_ref, fend_ref):
    rows, seq = ft_ref.shape
    r = lax.broadcasted_iota(jnp.int32, (CUMSUM_CHUNK, CUMSUM_CHUNK), 0)
    c = lax.broadcasted_iota(jnp.int32, (CUMSUM_CHUNK, CUMSUM_CHUNK), 1)
    upper = (r <= c).astype(F32)
    pad = jnp.zeros((LANES - F_PIECES * F_PIECE_STRIDE, CUMSUM_CHUNK), F32)
    carry = jnp.zeros((rows, 1), F32)
    for ci in range(seq // CUMSUM_CHUNK):
        sl = slice(ci * CUMSUM_CHUNK, (ci + 1) * CUMSUM_CHUNK)
        log_f = _log_sigmoid(ft_ref[:, sl] + b_ref[...])
        local = jnp.dot(log_f, upper, precision=lax.Precision.HIGHEST,
                        preferred_element_type=F32)
        rest = (local + carry) * LOG2_E
        carry = carry + local[:, CUMSUM_CHUNK - 1:CUMSUM_CHUNK]
        chunk_end = (ci + 1) * CUMSUM_CHUNK
        if chunk_end % FOX_TILE == 0:
            blk = chunk_end // FOX_TILE - 1
            fend_ref[:, blk:blk + 1] = rest[:, CUMSUM_CHUNK - 1:CUMSUM_CHUNK]
        pieces = []
        for _ in range(F_PIECES):
            piece = rest.astype(BF16).astype(F32)
            pieces.append(piece)
            rest = rest - piece
        kf_ref[sl, :] = jnp.concatenate(pieces + [pad], axis=0).T.astype(BF16)


def _forget_cumsum(ft2d, bias_col):
    rows, seq = ft2d.shape
    assert rows == F_PIECE_STRIDE
    return pl.pallas_call(
        _forget_cumsum_kernel,
        grid=(1,),
        in_specs=[pl.BlockSpec((rows, seq), lambda i: (0, 0)),
                  pl.BlockSpec((rows, 1), lambda i: (0, 0))],
        out_specs=[pl.BlockSpec((seq, LANES), lambda i: (0, 0)),
                   pl.BlockSpec((rows, seq // FOX_TILE), lambda i: (0, 0))],
        out_shape=[jax.ShapeDtypeStruct((seq, LANES), BF16),
                   jax.ShapeDtypeStruct((rows, seq // FOX_TILE), F32)],
        compiler_params=_compiler_params(("arbitrary",)),
        name="forget_cumsum",
    )(ft2d, bias_col)


def _head_row_masks():
    sub = lax.broadcasted_iota(jnp.int32, (LANES, 1), 0)
    first = sub < HEAD_DIM
    return (first, jnp.logical_not(first))


def _key_block(ref, j, tile):
    k0 = pl.multiple_of(j * tile, tile)
    return ref[0, pl.ds(k0, tile), :]


def _value_block(ref, j, tile):
    k0 = pl.multiple_of(j * tile, tile)
    return ref[0, :, pl.ds(k0, tile)]


def _fox_kernel(qt_ref, k_ref, vt_ref, kf_ref, fend_ref, o_ref, s_scr, kmax_scr):
    b, p, i = pl.program_id(0), pl.program_id(1), pl.program_id(2)
    tile = qt_ref.shape[2]
    heads = range(HEADS_PER_STEP)
    row_masks = _head_row_masks()
    qt = qt_ref[0]
    sub = lax.broadcasted_iota(jnp.int32, (LANES, 1), 0)
    f_rows_used = sub < F_PIECES * F_PIECE_STRIDE
    f_row = [(b * pl.num_programs(1) + p) * HEADS_PER_STEP + h for h in heads]

    @pl.when(i == 0)
    def _():
        lane = lax.broadcasted_iota(jnp.int32, (1, LANES), 1)
        col_max = jnp.max(jnp.abs(k_ref[0].astype(F32)), axis=0, keepdims=True)
        for h in heads:
            kmax_scr[h] = jnp.max(jnp.where(lane // HEAD_DIM == h, col_max, 0.0))

    qpt, qk_bound = [], []
    for h in heads:
        pick = jnp.logical_and(sub % F_PIECE_STRIDE == f_row[h], f_rows_used)
        f_sel = jnp.broadcast_to(jnp.where(pick, -1.0, 0.0).astype(BF16), qt.shape)
        q_head = jnp.where(row_masks[h], qt, jnp.zeros_like(qt))
        qpt.append(jnp.concatenate([q_head, f_sel], axis=0))
        q_l1 = jnp.sum(jnp.abs(q_head.astype(F32)), axis=0, keepdims=True)
        qk_bound.append(kmax_scr[h] * q_l1)
    key = lax.broadcasted_iota(jnp.int32, (tile, tile), 0)
    qry = lax.broadcasted_iota(jnp.int32, (tile, tile), 1)
    causal = key <= qry

    def issue(j, slot):
        k0 = pl.multiple_of(j * tile, tile)
        kp = jnp.concatenate([_key_block(k_ref, j, tile), kf_ref[pl.ds(k0, tile), :]], axis=1)
        for h in heads:
            s_scr[slot, h] = jnp.dot(kp, qpt[h], preferred_element_type=F32)

    def consume(j, slot, state, masked):
        vt = _value_block(vt_ref, j, tile)
        out = []
        for h in heads:
            m, acc = state[h]

            def scores():
                s = s_scr[slot, h]
                return jnp.where(causal, s, -jnp.inf) if masked else s

            v_aug = jnp.where(row_masks[h], vt, jnp.ones_like(vt))
            m_new = jnp.maximum(m, jnp.max(scores(), axis=0, keepdims=True))
            alpha = jnp.exp2(m - m_new)
            prob = jnp.exp2(scores() - m_new).astype(BF16)
            acc = alpha * acc + jnp.dot(v_aug, prob, preferred_element_type=F32)
            out.append((m_new, acc))
        return tuple(out)

    def still_live(st, next_j):
        live = False
        for h in heads:
            m, _ = st[h]
            headroom = jnp.max(qk_bound[h] - m) - fend_ref[f_row[h], next_j]
            live = jnp.logical_or(live, headroom >= -DEAD_COST_LOG2)
        return live

    issue(i, 0)
    issue(jnp.maximum(i - 1, 0), 1)
    state = tuple((jnp.full((1, tile), -jnp.inf, F32), jnp.zeros((LANES, tile), F32))
                  for _ in heads)
    state = consume(i, 0, state, True)

    def block_pair(carry):
        u, st, _ = carry
        j = i - 1 - 2 * u
        live_after = still_live(st, jnp.maximum(j - 2, 0))
        issue(j - 1, 0)
        st = consume(j, 1, st, False)
        issue(jnp.maximum(j - 2, 0), 1)
        return u + 1, consume(j - 1, 0, st, False), live_after

    n_pairs = i // 2
    _, state, live = lax.while_loop(
        lambda c: jnp.logical_and(c[0] < n_pairs, c[2]), block_pair,
        (jnp.int32(0), state, still_live(state, jnp.maximum(i - 1, 0))))
    state = lax.cond(jnp.logical_and(i % 2 == 1, live),
                     lambda st: consume(0, 1, st, False), lambda st: st, state)
    o = [acc / pltpu.roll(acc, HEAD_DIM, 0) for _, acc in state]
    o_ref[0] = jnp.where(row_masks[0], o[0], o[1]).T


def _sb_kernel(qt_ref, k_ref, vt_ref, o_ref, z_scr):
    i = pl.program_id(2)
    tile = qt_ref.shape[2]
    heads = range(z_scr.shape[1])
    row_masks = _head_row_masks()

    def pair_part(h):
        pair = h // HEADS_PER_STEP
        return slice(pair * LANES, (pair + 1) * LANES)

    qmt = []
    for h in heads:
        qt = qt_ref[0, pair_part(h), :]
        qmt.append(jnp.where(row_masks[h % HEADS_PER_STEP], qt, jnp.zeros_like(qt)))
    key = lax.broadcasted_iota(jnp.int32, (tile, tile), 0)
    qry = lax.broadcasted_iota(jnp.int32, (tile, tile), 1)
    strict = key < qry
    tr = lax.broadcasted_iota(jnp.int32, (SUFFIX_BLOCK, SUFFIX_BLOCK), 0)
    tc = lax.broadcasted_iota(jnp.int32, (SUFFIX_BLOCK, SUFFIX_BLOCK), 1)
    not_before = (tc >= tr).astype(BF16)
    n_sub = tile // SUFFIX_BLOCK

    def issue(j, slot):
        kj = _key_block(k_ref, j, tile)
        for h in heads:
            z_scr[slot, h] = jnp.dot(kj[:, pair_part(h)], qmt[h],
                                     preferred_element_type=F32)

    def log_weights(slot, h, masked):
        z = z_scr[slot, h]
        cost = jnp.maximum(z, 0.0) + jnp.log2(1.0 + jnp.exp2(-jnp.abs(z)))
        if masked:
            cost = jnp.where(strict, cost, 0.0)
        below = jnp.zeros((1, tile), F32)
        parts = [None] * n_sub
        for u in reversed(range(n_sub)):
            x = cost[u * SUFFIX_BLOCK:(u + 1) * SUFFIX_BLOCK]
            hi = x.astype(BF16)
            lo = (x - hi.astype(F32)).astype(BF16)
            inclusive = (jnp.dot(not_before, hi, preferred_element_type=F32)
                         + jnp.dot(not_before, lo, preferred_element_type=F32))
            parts[u] = inclusive + below
            below = below + inclusive[0:1, :]
        return z_scr[slot, h] - jnp.concatenate(parts, axis=0), below

    def accumulate(j, logs, state, masked, valid=None):
        vt = _value_block(vt_ref, j, tile)
        out = []
        for h in heads:
            r_cost, acc = state[h]
            log_w, block_cost = logs[h]
            w = jnp.exp2(log_w)
            if masked:
                w = jnp.where(strict, w, 0.0)
            pv = jnp.exp2(-r_cost) * jnp.dot(vt[pair_part(h)], w.astype(BF16),
                                             preferred_element_type=F32)
            if valid is not None:
                pv = jnp.where(valid, pv, 0.0)
                block_cost = jnp.where(valid, block_cost, 0.0)
            out.append((r_cost + block_cost, acc + pv))
        return tuple(out)

    def consume(j, slot, state):
        return accumulate(j, [log_weights(slot, h, False) for h in heads], state, False)

    def still_live(st):
        lowest = functools.reduce(jnp.minimum, [jnp.min(r_cost) for r_cost, _ in st])
        return lowest < DEAD_COST_LOG2

    prev = jnp.maximum(i - 1, 0)
    issue(i, 0)
    issue(prev, 1)
    state = tuple((jnp.zeros((1, tile), F32), jnp.zeros((LANES, tile), F32))
                  for _ in heads)
    logs_diag = [log_weights(0, h, True) for h in heads]
    logs_prev = [log_weights(1, h, False) for h in heads]
    state = accumulate(i, logs_diag, state, True)
    state = accumulate(prev, logs_prev, state, False, valid=i >= 1)
    issue(jnp.maximum(i - 2, 0), 0)

    def block_pair(carry):
        u, st, _ = carry
        j = i - 2 - 2 * u
        issue(j - 1, 1)
        st = consume(j, 0, st)

        def second(st):
            issue(jnp.maximum(j - 2, 0), 0)
            return consume(j - 1, 1, st)

        st = lax.cond(still_live(st), second, lambda st: st, st)
        return u + 1, st, still_live(st)

    n_left = jnp.maximum(i - 1, 0)
    _, state, live = lax.while_loop(lambda c: jnp.logical_and(c[0] < n_left // 2, c[2]),
                                    block_pair, (jnp.int32(0), state, still_live(state)))
    state = lax.cond(jnp.logical_and(n_left % 2 == 1, live),
                     lambda st: consume(0, 0, st), lambda st: st, state)
    pair_out = [jnp.where(row_masks[0], state[h][1], state[h + 1][1]).T
                for h in heads[::HEADS_PER_STEP]]
    o_ref[0] = jnp.concatenate(pair_out, axis=1)


def _attention(kernel, k3, t3, vmem_extra, smem_extra, extra_scratch, *, tile, pairs_per_step,
               k_blk, q_blk, v_blk, out_width, name):
    batch, seq, _ = k3.shape
    width = LANES * pairs_per_step
    n_steps = out_width // width
    assert out_width % width == 0 and all(x % pairs_per_step == 0 for x in (k_blk, q_blk, v_blk))
    k_blk, q_blk, v_blk = (x // pairs_per_step for x in (k_blk, q_blk, v_blk))
    in_specs = [
        pl.BlockSpec((1, width, tile), lambda b, p, i: (b, q_blk + p, i)),
        pl.BlockSpec((1, seq, width), lambda b, p, i: (b, 0, k_blk + p)),
        pl.BlockSpec((1, width, seq), lambda b, p, i: (b, v_blk + p, 0)),
    ]
    in_specs += [pl.BlockSpec(a.shape, lambda b, p, i: (0, 0)) for a in vmem_extra]
    in_specs += [pl.BlockSpec(memory_space=pltpu.SMEM) for _ in smem_extra]
    return pl.pallas_call(
        kernel,
        grid=(batch, n_steps, seq // tile),
        in_specs=in_specs,
        out_specs=pl.BlockSpec((1, tile, width), lambda b, p, i: (b, i, p)),
        out_shape=jax.ShapeDtypeStruct((batch, seq, out_width), F32),
        scratch_shapes=[pltpu.VMEM((2, HEADS_PER_STEP * pairs_per_step, tile, tile), F32)]
        + extra_scratch,
        compiler_params=_compiler_params(("parallel", "parallel", "arbitrary")),
        name=name,
    )(t3, k3, t3, *vmem_extra, *smem_extra)


def _mix_ffn_kernel(of_ref, os_ref, x_ref, ofh_ref, osh_ref, xh_ref, gf_ref, gs_ref, wout_ref,
                    g_ref, wup_ref, cw_ref, cb_ref, wdn_ref, gfin_ref, o_ref, act_ref, *,
                    tiles_per_seq, d_ff):
    i = pl.program_id(0)
    wf = of_ref.shape[1]

    def attn_residual(o_fox, o_sb, x_in):
        a = _rmsnorm_bf16(o_fox, gf_ref[...])
        b = _rmsnorm_bf16(o_sb, gs_ref[...])
        return x_in + (jnp.dot(a, wout_ref[:wf, :], preferred_element_type=F32)
                       + jnp.dot(b, wout_ref[wf:, :], preferred_element_type=F32))

    x = attn_residual(of_ref[...], os_ref[...], x_ref[...])
    x_halo = attn_residual(ofh_ref[...], osh_ref[...], xh_ref[...])
    g = g_ref[...]
    h = _rmsnorm_bf16(x, g)
    h_halo = _rmsnorm_bf16(x_halo, g)
    h_halo = jnp.where(i % tiles_per_seq == 0, jnp.zeros_like(h_halo), h_halo)
    h_ext = jnp.concatenate([h_halo, h], axis=0)

    def conv(u_ext, cols):
        out = cb_ref[:, cols] + cw_ref[CONV_WIDTH - 1:CONV_WIDTH, cols] * u_ext[BF16_SUBLANES:]
        for back in range(1, CONV_WIDTH):
            shifted = pltpu.roll(u_ext, back, 0)[BF16_SUBLANES:]
            tap = CONV_WIDTH - 1 - back
            out = out + cw_ref[tap:tap + 1, cols] * shifted
        return out

    def chunk_cols(c):
        return (slice(c * FFN_CHUNK, (c + 1) * FFN_CHUNK),
                slice(d_ff + c * FFN_CHUNK, d_ff + (c + 1) * FFN_CHUNK))

    def up_proj(c):
        return tuple(jnp.dot(h_ext, wup_ref[:, cols], preferred_element_type=F32)
                     for cols in chunk_cols(c))

    n_chunks = d_ff // FFN_CHUNK
    u_next = up_proj(0)
    for c in range(n_chunks):
        gate_cols, val_cols = chunk_cols(c)
        u_gate, u_val = u_next
        if c + 1 < n_chunks:
            u_next = up_proj(c + 1)
        gate = conv(u_gate, gate_cols)
        val = conv(u_val, val_cols)
        act_ref[:, gate_cols] = (gate / (1.0 + jnp.exp(-gate)) * val).astype(BF16)

    x2 = x + jnp.dot(act_ref[...], wdn_ref[...], preferred_element_type=F32)
    ms = jnp.mean(x2 * x2, axis=-1, keepdims=True)
    o_ref[...] = x2 * lax.rsqrt(ms + EPS) * gfin_ref[...]


def _mix_ffn(o_fox, o_sb, x2d, g_fox, g_sb, w_out, g, w_up, conv_w, conv_b, w_down, g_final, *,
             seq):
    n_rows, d = x2d.shape
    wf, ws = o_fox.shape[1], o_sb.shape[1]
    d_ff = w_down.shape[0]
    tiles_per_seq = seq // ROW_TILE
    halo_blocks_per_tile = ROW_TILE // BF16_SUBLANES
    tile_rows = lambda i: (i, 0)
    halo_rows = lambda i: (jnp.maximum(i * halo_blocks_per_tile - 1, 0), 0)
    whole = lambda i: (0, 0)
    resident = dict(pipeline_mode=pl.Buffered(1))
    return pl.pallas_call(
        functools.partial(_mix_ffn_kernel, tiles_per_seq=tiles_per_seq, d_ff=d_ff),
        grid=(n_rows // ROW_TILE,),
        in_specs=[
            pl.BlockSpec((ROW_TILE, wf), tile_rows),
            pl.BlockSpec((ROW_TILE, ws), tile_rows),
            pl.BlockSpec((ROW_TILE, d), tile_rows),
            pl.BlockSpec((BF16_SUBLANES, wf), halo_rows),
            pl.BlockSpec((BF16_SUBLANES, ws), halo_rows),
            pl.BlockSpec((BF16_SUBLANES, d), halo_rows),
            pl.BlockSpec((1, wf), whole),
            pl.BlockSpec((1, ws), whole),
            pl.BlockSpec((wf + ws, d), whole, **resident),
            pl.BlockSpec((1, d), whole),
            pl.BlockSpec((d, 2 * d_ff), whole, **resident),
            pl.BlockSpec((CONV_WIDTH, 2 * d_ff), whole),
            pl.BlockSpec((1, 2 * d_ff), whole),
            pl.BlockSpec((d_ff, d), whole, **resident),
            pl.BlockSpec((1, d), whole),
        ],
        out_specs=pl.BlockSpec((ROW_TILE, d), tile_rows),
        out_shape=jax.ShapeDtypeStruct((n_rows, d), F32),
        scratch_shapes=[pltpu.VMEM((ROW_TILE, d_ff), BF16)],
        compiler_params=_compiler_params(("parallel",)),
        name="mix_ffn",
    )(o_fox, o_sb, x2d, o_fox, o_sb, x2d, g_fox, g_sb, w_out, g, w_up, conv_w, conv_b, w_down,
      g_final)


def kernel(x, attn_norm_g, w_in, forget_bias, fox_out_g, sb_out_g, w_out, ffn_norm_g, w_up,
           conv_w, conv_b, w_down, final_norm_g):
    batch, seq, d = x.shape
    depth = w_in.shape[0]
    n_fox = forget_bias.shape[1]
    fox_w = fox_out_g.shape[1]
    sb_w = sb_out_g.shape[1]
    assert seq % ROW_TILE == 0 and seq % FOX_TILE == 0 and seq % SB_TILE == 0
    assert SB_TILE % SUFFIX_BLOCK == 0
    assert fox_w == n_fox * HEAD_DIM and n_fox % HEADS_PER_STEP == 0
    assert fox_w % PROJ_CHUNK == 0 and sb_w % PROJ_CHUNK == 0
    assert w_down.shape[1] % FFN_CHUNK == 0

    scale = LOG2_E * HEAD_DIM ** -0.5
    fox_blocks = fox_w // LANES
    sb_blocks = sb_w // LANES
    c_fq, c_fk, c_fv, c_fl = 0, fox_w, 2 * fox_w, 3 * fox_w
    c_sq = c_fl + n_fox
    c_sk, c_sv = c_sq + sb_w, c_sq + 2 * sb_w

    x2d = x.reshape(batch * seq, d)
    for l in range(depth):
        w_l = w_in[l]
        cols = lambda start, width: w_l[:, start:start + width]
        w_k = jnp.concatenate([cols(c_fk, fox_w), cols(c_sk, sb_w)], axis=1).astype(BF16)
        w_t = jnp.concatenate([cols(c_fq, fox_w), cols(c_sq, sb_w),
                               cols(c_fv, fox_w), cols(c_sv, sb_w)], axis=1).T.astype(BF16)
        wft = cols(c_fl, n_fox).T.astype(BF16)
        k2d, t3, f_logit_t = _in_proj(x2d, attn_norm_g[l][None, :], w_k, w_t, wft,
                                      batch=batch, seq=seq, q_rows=fox_w + sb_w, scale=scale)
        bias_col = jnp.tile(forget_bias[l], batch)[:, None]
        kf, f_end = _forget_cumsum(f_logit_t.reshape(batch * n_fox, seq), bias_col)
        k3 = k2d.reshape(batch, seq, k2d.shape[1])

        o_fox = _attention(_fox_kernel, k3, t3, [kf], [f_end],
                           [pltpu.SMEM((HEADS_PER_STEP,), F32)], tile=FOX_TILE, pairs_per_step=1,
                           k_blk=0, q_blk=0, v_blk=fox_blocks + sb_blocks, out_width=fox_w,
                           name="fox_attn")
        o_sb = _attention(_sb_kernel, k3, t3, [], [], [], tile=SB_TILE,
                          pairs_per_step=SB_PAIRS_PER_STEP, k_blk=fox_blocks, q_blk=fox_blocks,
                          v_blk=2 * fox_blocks + sb_blocks, out_width=sb_w, name="sb_attn")

        assert l == depth - 1, "depth > 1 needs an un-normalised output between layers"
        x2d = _mix_ffn(o_fox.reshape(batch * seq, fox_w), o_sb.reshape(batch * seq, sb_w), x2d,
                       fox_out_g[l][None, :], sb_out_g[l][None, :], w_out[l].astype(BF16),
                       ffn_norm_g[l][None, :], w_up[l].astype(BF16), conv_w[l],
                       conv_b[l][None, :], w_down[l].astype(BF16), final_norm_g[None, :], seq=seq)
    return x2d.reshape(batch, seq, d)
```

```python
import functools

import jax
import jax.numpy as jnp
from jax import lax
from jax.experimental import pallas as pl
from jax.experimental.pallas import tpu as pltpu

HEAD_DIM = 64
EPS = 1e-6
CONV_WIDTH = 3

LANES = 128
HEADS_PER_STEP = LANES // HEAD_DIM
BF16_SUBLANES = 16
VMEM_LIMIT_BYTES = 56 * 1024 * 1024

ROW_TILE = 512
PROJ_CHUNK = 512
FOX_TILE = 512
SB_TILE = 256
SB_PAIRS_PER_STEP = 2
FOX_PAIRS_PER_STEP = 1
FOX_BLOCKS_PER_TRIP = 4
SUFFIX_BLOCK = 256
CUMSUM_CHUNK = 256
FFN_CHUNK = 256
F_PIECES = 3
F_PIECE_STRIDE = 32

F32 = jnp.float32
BF16 = jnp.bfloat16
_NT = (((1,), (1,)), ((), ()))
LOG2_E = 1.4426950408889634
DEAD_COST_LOG2 = 160.0


def _compiler_params(semantics):
    return pltpu.CompilerParams(dimension_semantics=semantics,
                                vmem_limit_bytes=VMEM_LIMIT_BYTES)


def _rmsnorm_bf16(x, g):
    ms = jnp.mean(x * x, axis=-1, keepdims=True)
    return (x * lax.rsqrt(ms + EPS) * g).astype(BF16)


def _log_sigmoid(x):
    return jnp.minimum(x, 0.0) - jnp.log1p(jnp.exp(-jnp.abs(x)))


def _in_proj_kernel(x_ref, g_ref, wk_ref, wt_ref, wft_ref, k_ref, t_ref, ft_ref, *,
                    q_rows, scale):
    h = _rmsnorm_bf16(x_ref[...], g_ref[...])
    for c in range(wk_ref.shape[1] // PROJ_CHUNK):
        sl = slice(c * PROJ_CHUNK, (c + 1) * PROJ_CHUNK)
        k_ref[:, sl] = jnp.dot(h, wk_ref[:, sl], preferred_element_type=F32).astype(BF16)
    for c in range(wt_ref.shape[0] // PROJ_CHUNK):
        sl = slice(c * PROJ_CHUNK, (c + 1) * PROJ_CHUNK)
        y = lax.dot_general(wt_ref[sl, :], h, _NT, preferred_element_type=F32)
        if (c + 1) * PROJ_CHUNK <= q_rows:
            y = y * scale
        t_ref[0, sl, :] = y.astype(BF16)
    ft_ref[0] = lax.dot_general(wft_ref[...], h, _NT, preferred_element_type=F32)


def _in_proj(x2d, g, w_k, w_t, wft, *, batch, seq, q_rows, scale):
    n_rows, d = x2d.shape
    k_cols = w_k.shape[1]
    t_rows = w_t.shape[0]
    n_heads = wft.shape[0]
    tiles_per_seq = seq // ROW_TILE
    pos_block = lambda i: (i // tiles_per_seq, 0, i % tiles_per_seq)
    return pl.pallas_call(
        functools.partial(_in_proj_kernel, q_rows=q_rows, scale=scale),
        grid=(n_rows // ROW_TILE,),
        in_specs=[
            pl.BlockSpec((ROW_TILE, d), lambda i: (i, 0)),
            pl.BlockSpec((1, d), lambda i: (0, 0)),
            pl.BlockSpec((d, k_cols), lambda i: (0, 0)),
            pl.BlockSpec((t_rows, d), lambda i: (0, 0)),
            pl.BlockSpec((n_heads, d), lambda i: (0, 0)),
        ],
        out_specs=[
            pl.BlockSpec((ROW_TILE, k_cols), lambda i: (i, 0)),
            pl.BlockSpec((1, t_rows, ROW_TILE), pos_block),
            pl.BlockSpec((1, n_heads, ROW_TILE), pos_block),
        ],
        out_shape=[
            jax.ShapeDtypeStruct((n_rows, k_cols), BF16),
            jax.ShapeDtypeStruct((batch, t_rows, seq), BF16),
            jax.ShapeDtypeStruct((batch, n_heads, seq), F32),
        ],
        compiler_params=_compiler_params(("parallel",)),
        name="in_proj",
    )(x2d, g, w_k, w_t, wft)


def _forget_cumsum_kernel(ft_ref, b_ref, kf_ref, fend_ref):
    rows, seq = ft_ref.shape
    r = lax.broadcasted_iota(jnp.int32, (CUMSUM_CHUNK, CUMSUM_CHUNK), 0)
    c = lax.broadcasted_iota(jnp.int32, (CUMSUM_CHUNK, CUMSUM_CHUNK), 1)
    upper = (r <= c).astype(F32)
    pad = jnp.zeros((LANES - F_PIECES * F_PIECE_STRIDE, CUMSUM_CHUNK), F32)
    carry = jnp.zeros((rows, 1), F32)
    for ci in range(seq // CUMSUM_CHUNK):
        sl = slice(ci * CUMSUM_CHUNK, (ci + 1) * CUMSUM_CHUNK)
        log_f = _log_sigmoid(ft_ref[:, sl] + b_ref[...])
        local = jnp.dot(log_f, upper, precision=lax.Precision.HIGHEST,
                        preferred_element_type=F32)
        rest = (local + carry) * LOG2_E
        carry = carry + local[:, CUMSUM_CHUNK - 1:CUMSUM_CHUNK]
        chunk_end = (ci + 1) * CUMSUM_CHUNK
        if chunk_end % FOX_TILE == 0:
            blk = chunk_end // FOX_TILE - 1
            fend_ref[:, blk:blk + 1] = rest[:, CUMSUM_CHUNK - 1:CUMSUM_CHUNK]
        pieces = []
        for _ in range(F_PIECES):
            piece = rest.astype(BF16).astype(F32)
            pieces.append(piece)
            rest = rest - piece
        kf_ref[sl, :] = jnp.concatenate(pieces + [pad], axis=0).T.astype(BF16)


def _forget_cumsum(ft2d, bias_col):
    rows, seq = ft2d.shape
    assert rows == F_PIECE_STRIDE
    return pl.pallas_call(
        _forget_cumsum_kernel,
        grid=(1,),
        in_specs=[pl.BlockSpec((rows, seq), lambda i: (0, 0)),
                  pl.BlockSpec((rows, 1), lambda i: (0, 0))],
        out_specs=[pl.BlockSpec((seq, LANES), lambda i: (0, 0)),
                   pl.BlockSpec((rows, seq // FOX_TILE), lambda i: (0, 0))],
        out_shape=[jax.ShapeDtypeStruct((seq, LANES), BF16),
                   jax.ShapeDtypeStruct((rows, seq // FOX_TILE), F32)],
        compiler_params=_compiler_params(("arbitrary",)),
        name="forget_cumsum",
    )(ft2d, bias_col)


def _head_row_masks():
    sub = lax.broadcasted_iota(jnp.int32, (LANES, 1), 0)
    first = sub < HEAD_DIM
    return (first, jnp.logical_not(first))


def _key_block(ref, j, tile):
    k0 = pl.multiple_of(j * tile, tile)
    return ref[0, pl.ds(k0, tile), :]


def _value_block(ref, j, tile):
    k0 = pl.multiple_of(j * tile, tile)
    return ref[0, :, pl.ds(k0, tile)]


def _fox_kernel(qt_ref, k_ref, vt_ref, kf_ref, fend_ref, o_ref, s_scr, kmax_scr):
    b, p, i = pl.program_id(0), pl.program_id(1), pl.program_id(2)
    tile = qt_ref.shape[2]
    heads = range(s_scr.shape[1])
    row_masks = _head_row_masks()
    sub = lax.broadcasted_iota(jnp.int32, (LANES, 1), 0)
    f_rows_used = sub < F_PIECES * F_PIECE_STRIDE
    f_row = [(b * pl.num_programs(1) + p) * len(heads) + h for h in heads]

    def pair_part(h):
        pair = h // HEADS_PER_STEP
        return slice(pair * LANES, (pair + 1) * LANES)

    @pl.when(i == 0)
    def _():
        lane = lax.broadcasted_iota(jnp.int32, (1, k_ref.shape[2]), 1)
        col_max = jnp.max(jnp.abs(k_ref[0].astype(F32)), axis=0, keepdims=True)
        for h in heads:
            kmax_scr[h] = jnp.max(jnp.where(lane // HEAD_DIM == h, col_max, 0.0))

    qpt, qk_bound = [], []
    for h in heads:
        qt = qt_ref[0, pair_part(h), :]
        pick = jnp.logical_and(sub % F_PIECE_STRIDE == f_row[h], f_rows_used)
        f_sel = jnp.broadcast_to(jnp.where(pick, -1.0, 0.0).astype(BF16), qt.shape)
        q_head = jnp.where(row_masks[h % HEADS_PER_STEP], qt, jnp.zeros_like(qt))
        qpt.append(jnp.concatenate([q_head, f_sel], axis=0))
        q_l1 = jnp.sum(jnp.abs(q_head.astype(F32)), axis=0, keepdims=True)
        qk_bound.append(kmax_scr[h] * q_l1)
    key = lax.broadcasted_iota(jnp.int32, (tile, tile), 0)
    qry = lax.broadcasted_iota(jnp.int32, (tile, tile), 1)
    causal = key <= qry

    def issue(j, slot):
        k0 = pl.multiple_of(j * tile, tile)
        kj = _key_block(k_ref, j, tile)
        kfj = kf_ref[pl.ds(k0, tile), :]
        for h in heads:
            kp = jnp.concatenate([kj[:, pair_part(h)], kfj], axis=1)
            s_scr[slot, h] = jnp.dot(kp, qpt[h], preferred_element_type=F32)

    def consume(j, slot, state, masked):
        vt_all = _value_block(vt_ref, j, tile)
        out = []
        for h in heads:
            m, acc = state[h]
            vt = vt_all[pair_part(h)]

            def scores():
                s = s_scr[slot, h]
                return jnp.where(causal, s, -jnp.inf) if masked else s

            v_aug = jnp.where(row_masks[h % HEADS_PER_STEP], vt, jnp.ones_like(vt))
            m_new = jnp.maximum(m, jnp.max(scores(), axis=0, keepdims=True))
            alpha = jnp.exp2(m - m_new)
            prob = jnp.exp2(scores() - m_new).astype(BF16)
            acc = alpha * acc + jnp.dot(v_aug, prob, preferred_element_type=F32)
            out.append((m_new, acc))
        return tuple(out)

    def still_live(st, next_j):
        live = False
        for h in heads:
            m, _ = st[h]
            headroom = jnp.max(qk_bound[h] - m) - fend_ref[f_row[h], next_j]
            live = jnp.logical_or(live, headroom >= -DEAD_COST_LOG2)
        return live

    issue(i, 0)
    issue(jnp.maximum(i - 1, 0), 1)
    state = tuple((jnp.full((1, tile), -jnp.inf, F32), jnp.zeros((LANES, tile), F32))
                  for _ in heads)
    state = consume(i, 0, state, True)

    def sweep(first_j, n_trips, blocks_per_trip, state, live):
        def trip(carry):
            u, st, _ = carry
            j = first_j - blocks_per_trip * u
            live_after = still_live(st, jnp.maximum(j - blocks_per_trip, 0))
            for t in range(blocks_per_trip):
                issue(jnp.maximum(j - t - 1, 0), t % 2)
                st = consume(j - t, 1 - t % 2, st, False)
            return u + 1, st, live_after

        _, state, live = lax.while_loop(lambda c: jnp.logical_and(c[0] < n_trips, c[2]), trip,
                                        (jnp.int32(0), state, live))
        return state, live

    live = still_live(state, jnp.maximum(i - 1, 0))
    n_long = i // FOX_BLOCKS_PER_TRIP
    state, live = sweep(i - 1, n_long, FOX_BLOCKS_PER_TRIP, state, live)
    rest = i - n_long * FOX_BLOCKS_PER_TRIP
    state, live = sweep(rest - 1, rest // 2, 2, state, live)
    state = lax.cond(jnp.logical_and(rest % 2 == 1, live),
                     lambda st: consume(0, 1, st, False), lambda st: st, state)
    o = [acc / pltpu.roll(acc, HEAD_DIM, 0) for _, acc in state]
    pair_out = [jnp.where(row_masks[0], o[h], o[h + 1]).T for h in heads[::HEADS_PER_STEP]]
    o_ref[0] = jnp.concatenate(pair_out, axis=1)


def _sb_kernel(qt_ref, k_ref, vt_ref, o_ref, z_scr):
    i = pl.program_id(2)
    tile = qt_ref.shape[2]
    heads = range(z_scr.shape[1])
    row_masks = _head_row_masks()

    def pair_part(h):
        pair = h // HEADS_PER_STEP
        return slice(pair * LANES, (pair + 1) * LANES)

    qmt = []
    for h in heads:
        qt = qt_ref[0, pair_part(h), :]
        qmt.append(jnp.where(row_masks[h % HEADS_PER_STEP], qt, jnp.zeros_like(qt)))
    key = lax.broadcasted_iota(jnp.int32, (tile, tile), 0)
    qry = lax.broadcasted_iota(jnp.int32, (tile, tile), 1)
    strict = key < qry
    tr = lax.broadcasted_iota(jnp.int32, (SUFFIX_BLOCK, SUFFIX_BLOCK), 0)
    tc = lax.broadcasted_iota(jnp.int32, (SUFFIX_BLOCK, SUFFIX_BLOCK), 1)
    not_before = (tc >= tr).astype(BF16)
    n_sub = tile // SUFFIX_BLOCK

    def issue(j, slot):
        kj = _key_block(k_ref, j, tile)
        for h in heads:
            z_scr[slot, h] = jnp.dot(kj[:, pair_part(h)], qmt[h],
                                     preferred_element_type=F32)

    def log_weights(slot, h, masked):
        z = z_scr[slot, h]
        cost = jnp.maximum(z, 0.0) + jnp.log2(1.0 + jnp.exp2(-jnp.abs(z)))
        if masked:
            cost = jnp.where(strict, cost, 0.0)
        below = jnp.zeros((1, tile), F32)
        parts = [None] * n_sub
        for u in reversed(range(n_sub)):
            x = cost[u * SUFFIX_BLOCK:(u + 1) * SUFFIX_BLOCK]
            hi = x.astype(BF16)
            lo = (x - hi.astype(F32)).astype(BF16)
            inclusive = (jnp.dot(not_before, hi, preferred_element_type=F32)
                         + jnp.dot(not_before, lo, preferred_element_type=F32))
            parts[u] = inclusive + below
            below = below + inclusive[0:1, :]
        return z_scr[slot, h] - jnp.concatenate(parts, axis=0), below

    def accumulate(j, logs, state, masked, valid=None):
        vt = _value_block(vt_ref, j, tile)
        out = []
        for h in heads:
            r_cost, acc = state[h]
            log_w, block_cost = logs[h]
            w = jnp.exp2(log_w)
            if masked:
                w = jnp.where(strict, w, 0.0)
            pv = jnp.exp2(-r_cost) * jnp.dot(vt[pair_part(h)], w.astype(BF16),
                                             preferred_element_type=F32)
            if valid is not None:
                pv = jnp.where(valid, pv, 0.0)
                block_cost = jnp.where(valid, block_cost, 0.0)
            out.append((r_cost + block_cost, acc + pv))
        return tuple(out)

    def consume(j, slot, state):
        return accumulate(j, [log_weights(slot, h, False) for h in heads], state, False)

    def still_live(st):
        lowest = functools.reduce(jnp.minimum, [jnp.min(r_cost) for r_cost, _ in st])
        return lowest < DEAD_COST_LOG2

    prev = jnp.maximum(i - 1, 0)
    issue(i, 0)
    issue(prev, 1)
    state = tuple((jnp.zeros((1, tile), F32), jnp.zeros((LANES, tile), F32))
                  for _ in heads)
    logs_diag = [log_weights(0, h, True) for h in heads]
    logs_prev = [log_weights(1, h, False) for h in heads]
    state = accumulate(i, logs_diag, state, True)
    state = accumulate(prev, logs_prev, state, False, valid=i >= 1)
    issue(jnp.maximum(i - 2, 0), 0)

    def block_pair(carry):
        u, st, _ = carry
        j = i - 2 - 2 * u
        issue(j - 1, 1)
        st = consume(j, 0, st)

        def second(st):
            issue(jnp.maximum(j - 2, 0), 0)
            return consume(j - 1, 1, st)

        st = lax.cond(still_live(st), second, lambda st: st, st)
        return u + 1, st, still_live(st)

    n_left = jnp.maximum(i - 1, 0)
    _, state, live = lax.while_loop(lambda c: jnp.logical_and(c[0] < n_left // 2, c[2]),
                                    block_pair, (jnp.int32(0), state, still_live(state)))
    state = lax.cond(jnp.logical_and(n_left % 2 == 1, live),
                     lambda st: consume(0, 0, st), lambda st: st, state)
    pair_out = [jnp.where(row_masks[0], state[h][1], state[h + 1][1]).T
                for h in heads[::HEADS_PER_STEP]]
    o_ref[0] = jnp.concatenate(pair_out, axis=1)


def _attention(kernel, k3, t3, vmem_extra, smem_extra, extra_scratch, *, tile, pairs_per_step,
               k_blk, q_blk, v_blk, out_width, name):
    batch, seq, _ = k3.shape
    width = LANES * pairs_per_step
    n_steps = out_width // width
    assert out_width % width == 0 and all(x % pairs_per_step == 0 for x in (k_blk, q_blk, v_blk))
    k_blk, q_blk, v_blk = (x // pairs_per_step for x in (k_blk, q_blk, v_blk))
    in_specs = [
        pl.BlockSpec((1, width, tile), lambda b, p, i: (b, q_blk + p, i)),
        pl.BlockSpec((1, seq, width), lambda b, p, i: (b, 0, k_blk + p)),
        pl.BlockSpec((1, width, seq), lambda b, p, i: (b, v_blk + p, 0)),
    ]
    in_specs += [pl.BlockSpec(a.shape, lambda b, p, i: (0, 0)) for a in vmem_extra]
    in_specs += [pl.BlockSpec(memory_space=pltpu.SMEM) for _ in smem_extra]
    return pl.pallas_call(
        kernel,
        grid=(batch, n_steps, seq // tile),
        in_specs=in_specs,
        out_specs=pl.BlockSpec((1, tile, width), lambda b, p, i: (b, i, p)),
        out_shape=jax.ShapeDtypeStruct((batch, seq, out_width), F32),
        scratch_shapes=[pltpu.VMEM((2, HEADS_PER_STEP * pairs_per_step, tile, tile), F32)]
        + extra_scratch,
        compiler_params=_compiler_params(("parallel", "parallel", "arbitrary")),
        name=name,
    )(t3, k3, t3, *vmem_extra, *smem_extra)


def _mix_ffn_kernel(of_ref, os_ref, x_ref, ofh_ref, osh_ref, xh_ref, gf_ref, gs_ref, wout_ref,
                    g_ref, wup_ref, cw_ref, cb_ref, wdn_ref, gfin_ref, o_ref, act_ref, *,
                    tiles_per_seq, d_ff):
    i = pl.program_id(0)
    wf = of_ref.shape[1]

    def attn_residual(o_fox, o_sb, x_in):
        a = _rmsnorm_bf16(o_fox, gf_ref[...])
        b = _rmsnorm_bf16(o_sb, gs_ref[...])
        return x_in + (jnp.dot(a, wout_ref[:wf, :], preferred_element_type=F32)
                       + jnp.dot(b, wout_ref[wf:, :], preferred_element_type=F32))

    x = attn_residual(of_ref[...], os_ref[...], x_ref[...])
    x_halo = attn_residual(ofh_ref[...], osh_ref[...], xh_ref[...])
    g = g_ref[...]
    h = _rmsnorm_bf16(x, g)
    h_halo = _rmsnorm_bf16(x_halo, g)
    h_halo = jnp.where(i % tiles_per_seq == 0, jnp.zeros_like(h_halo), h_halo)
    h_ext = jnp.concatenate([h_halo, h], axis=0)

    def conv(u_ext, cols):
        out = cb_ref[:, cols] + cw_ref[CONV_WIDTH - 1:CONV_WIDTH, cols] * u_ext[BF16_SUBLANES:]
        for back in range(1, CONV_WIDTH):
            shifted = pltpu.roll(u_ext, back, 0)[BF16_SUBLANES:]
            tap = CONV_WIDTH - 1 - back
            out = out + cw_ref[tap:tap + 1, cols] * shifted
        return out

    def chunk_cols(c):
        return (slice(c * FFN_CHUNK, (c + 1) * FFN_CHUNK),
                slice(d_ff + c * FFN_CHUNK, d_ff + (c + 1) * FFN_CHUNK))

    def up_proj(c):
        return tuple(jnp.dot(h_ext, wup_ref[:, cols], preferred_element_type=F32)
                     for cols in chunk_cols(c))

    n_chunks = d_ff // FFN_CHUNK
    u_next = up_proj(0)
    for c in range(n_chunks):
        gate_cols, val_cols = chunk_cols(c)
        u_gate, u_val = u_next
        if c + 1 < n_chunks:
            u_next = up_proj(c + 1)
        gate = conv(u_gate, gate_cols)
        val = conv(u_val, val_cols)
        act_ref[:, gate_cols] = (gate / (1.0 + jnp.exp(-gate)) * val).astype(BF16)

    x2 = x + jnp.dot(act_ref[...], wdn_ref[...], preferred_element_type=F32)
    ms = jnp.mean(x2 * x2, axis=-1, keepdims=True)
    o_ref[...] = x2 * lax.rsqrt(ms + EPS) * gfin_ref[...]


def _mix_ffn(o_fox, o_sb, x2d, g_fox, g_sb, w_out, g, w_up, conv_w, conv_b, w_down, g_final, *,
             seq):
    n_rows, d = x2d.shape
    wf, ws = o_fox.shape[1], o_sb.shape[1]
    d_ff = w_down.shape[0]
    tiles_per_seq = seq // ROW_TILE
    halo_blocks_per_tile = ROW_TILE // BF16_SUBLANES
    tile_rows = lambda i: (i, 0)
    halo_rows = lambda i: (jnp.maximum(i * halo_blocks_per_tile - 1, 0), 0)
    whole = lambda i: (0, 0)
    resident = dict(pipeline_mode=pl.Buffered(1))
    return pl.pallas_call(
        functools.partial(_mix_ffn_kernel, tiles_per_seq=tiles_per_seq, d_ff=d_ff),
        grid=(n_rows // ROW_TILE,),
        in_specs=[
            pl.BlockSpec((ROW_TILE, wf), tile_rows),
            pl.BlockSpec((ROW_TILE, ws), tile_rows),
            pl.BlockSpec((ROW_TILE, d), tile_rows),
            pl.BlockSpec((BF16_SUBLANES, wf), halo_rows),
            pl.BlockSpec((BF16_SUBLANES, ws), halo_rows),
            pl.BlockSpec((BF16_SUBLANES, d), halo_rows),
            pl.BlockSpec((1, wf), whole),
            pl.BlockSpec((1, ws), whole),
            pl.BlockSpec((wf + ws, d), whole, **resident),
            pl.BlockSpec((1, d), whole),
            pl.BlockSpec((d, 2 * d_ff), whole, **resident),
            pl.BlockSpec((CONV_WIDTH, 2 * d_ff), whole),
            pl.BlockSpec((1, 2 * d_ff), whole),
            pl.BlockSpec((d_ff, d), whole, **resident),
            pl.BlockSpec((1, d), whole),
        ],
        out_specs=pl.BlockSpec((ROW_TILE, d), tile_rows),
        out_shape=jax.ShapeDtypeStruct((n_rows, d), F32),
        scratch_shapes=[pltpu.VMEM((ROW_TILE, d_ff), BF16)],
        compiler_params=_compiler_params(("parallel",)),
        name="mix_ffn",
    )(o_fox, o_sb, x2d, o_fox, o_sb, x2d, g_fox, g_sb, w_out, g, w_up, conv_w, conv_b, w_down,
      g_final)


def kernel(x, attn_norm_g, w_in, forget_bias, fox_out_g, sb_out_g, w_out, ffn_norm_g, w_up,
           conv_w, conv_b, w_down, final_norm_g):
    batch, seq, d = x.shape
    depth = w_in.shape[0]
    n_fox = forget_bias.shape[1]
    fox_w = fox_out_g.shape[1]
    sb_w = sb_out_g.shape[1]
    assert seq % ROW_TILE == 0 and seq % FOX_TILE == 0 and seq % SB_TILE == 0
    assert SB_TILE % SUFFIX_BLOCK == 0
    assert fox_w == n_fox * HEAD_DIM and n_fox % HEADS_PER_STEP == 0
    assert fox_w % PROJ_CHUNK == 0 and sb_w % PROJ_CHUNK == 0
    assert w_down.shape[1] % FFN_CHUNK == 0

    scale = LOG2_E * HEAD_DIM ** -0.5
    fox_blocks = fox_w // LANES
    sb_blocks = sb_w // LANES
    c_fq, c_fk, c_fv, c_fl = 0, fox_w, 2 * fox_w, 3 * fox_w
    c_sq = c_fl + n_fox
    c_sk, c_sv = c_sq + sb_w, c_sq + 2 * sb_w

    x2d = x.reshape(batch * seq, d)
    for l in range(depth):
        w_l = w_in[l]
        cols = lambda start, width: w_l[:, start:start + width]
        w_k = jnp.concatenate([cols(c_fk, fox_w), cols(c_sk, sb_w)], axis=1).astype(BF16)
        w_t = jnp.concatenate([cols(c_fq, fox_w), cols(c_sq, sb_w),
                               cols(c_fv, fox_w), cols(c_sv, sb_w)], axis=1).T.astype(BF16)
        wft = cols(c_fl, n_fox).T.astype(BF16)
        k2d, t3, f_logit_t = _in_proj(x2d, attn_norm_g[l][None, :], w_k, w_t, wft,
                                      batch=batch, seq=seq, q_rows=fox_w + sb_w, scale=scale)
        bias_col = jnp.tile(forget_bias[l], batch)[:, None]
        kf, f_end = _forget_cumsum(f_logit_t.reshape(batch * n_fox, seq), bias_col)
        k3 = k2d.reshape(batch, seq, k2d.shape[1])

        o_fox = _attention(_fox_kernel, k3, t3, [kf], [f_end],
                           [pltpu.SMEM((HEADS_PER_STEP * FOX_PAIRS_PER_STEP,), F32)],
                           tile=FOX_TILE, pairs_per_step=FOX_PAIRS_PER_STEP, k_blk=0, q_blk=0,
                           v_blk=fox_blocks + sb_blocks, out_width=fox_w, name="fox_attn")
        o_sb = _attention(_sb_kernel, k3, t3, [], [], [], tile=SB_TILE,
                          pairs_per_step=SB_PAIRS_PER_STEP, k_blk=fox_blocks, q_blk=fox_blocks,
                          v_blk=2 * fox_blocks + sb_blocks, out_width=sb_w, name="sb_attn")

        assert l == depth - 1, "depth > 1 needs an un-normalised output between layers"
        x2d = _mix_ffn(o_fox.reshape(batch * seq, fox_w), o_sb.reshape(batch * seq, sb_w), x2d,
                       fox_out_g[l][None, :], sb_out_g[l][None, :], w_out[l].astype(BF16),
                       ffn_norm_g[l][None, :], w_up[l].astype(BF16), conv_w[l],
                       conv_b[l][None, :], w_down[l].astype(BF16), final_norm_g[None, :], seq=seq)
    return x2d.reshape(batch, seq, d)
```

```python
import functools

import jax
import jax.numpy as jnp
from jax import lax
from jax.experimental import pallas as pl
from jax.experimental.pallas import tpu as pltpu

HEAD_DIM = 64
EPS = 1e-6
CONV_WIDTH = 3

LANES = 128
HEADS_PER_STEP = LANES // HEAD_DIM
BF16_SUBLANES = 16
VMEM_LIMIT_BYTES = 56 * 1024 * 1024

ROW_TILE = 512
PROJ_CHUNK = 512
FOX_TILE = 512
SB_TILE = 256
SB_PAIRS_PER_STEP = 2
FOX_PAIRS_PER_STEP = 1
FOX_BLOCKS_PER_TRIP = 4
SUFFIX_BLOCK = 256
CUMSUM_CHUNK = 256
FFN_CHUNK = 256
F_PIECES = 3
F_PIECE_STRIDE = 32

F32 = jnp.float32
BF16 = jnp.bfloat16
_NT = (((1,), (1,)), ((), ()))
LOG2_E = 1.4426950408889634
DEAD_COST_LOG2 = 160.0


def _compiler_params(semantics):
    return pltpu.CompilerParams(dimension_semantics=semantics,
                                vmem_limit_bytes=VMEM_LIMIT_BYTES)


def _rmsnorm_bf16(x, g):
    ms = jnp.mean(x * x, axis=-1, keepdims=True)
    return (x * lax.rsqrt(ms + EPS) * g).astype(BF16)


def _log_sigmoid(x):
    return jnp.minimum(x, 0.0) - jnp.log1p(jnp.exp(-jnp.abs(x)))


def _in_proj_kernel(x_ref, g_ref, wk_ref, wt_ref, wft_ref, k_ref, t_ref, ft_ref, *,
                    q_rows, scale):
    h = _rmsnorm_bf16(x_ref[...], g_ref[...])
    for c in range(wk_ref.shape[1] // PROJ_CHUNK):
        sl = slice(c * PROJ_CHUNK, (c + 1) * PROJ_CHUNK)
        k_ref[:, sl] = jnp.dot(h, wk_ref[:, sl], preferred_element_type=F32).astype(BF16)
    for c in range(wt_ref.shape[0] // PROJ_CHUNK):
        sl = slice(c * PROJ_CHUNK, (c + 1) * PROJ_CHUNK)
        y = lax.dot_general(wt_ref[sl, :], h, _NT, preferred_element_type=F32)
        if (c + 1) * PROJ_CHUNK <= q_rows:
            y = y * scale
        t_ref[0, sl, :] = y.astype(BF16)
    ft_ref[0] = lax.dot_general(wft_ref[...], h, _NT, preferred_element_type=F32)


def _in_proj(x2d, g, w_k, w_t, wft, *, batch, seq, q_rows, scale):
    n_rows, d = x2d.shape
    k_cols = w_k.shape[1]
    t_rows = w_t.shape[0]
    n_heads = wft.shape[0]
    tiles_per_seq = seq // ROW_TILE
    pos_block = lambda i: (i // tiles_per_seq, 0, i % tiles_per_seq)
    return pl.pallas_call(
        functools.partial(_in_proj_kernel, q_rows=q_rows, scale=scale),
        grid=(n_rows // ROW_TILE,),
        in_specs=[
            pl.BlockSpec((ROW_TILE, d), lambda i: (i, 0)),
            pl.BlockSpec((1, d), lambda i: (0, 0)),
            pl.BlockSpec((d, k_cols), lambda i: (0, 0)),
            pl.BlockSpec((t_rows, d), lambda i: (0, 0)),
            pl.BlockSpec((n_heads, d), lambda i: (0, 0)),
        ],
        out_specs=[
            pl.BlockSpec((ROW_TILE, k_cols), lambda i: (i, 0)),
            pl.BlockSpec((1, t_rows, ROW_TILE), pos_block),
            pl.BlockSpec((1, n_heads, ROW_TILE), pos_block),
        ],
        out_shape=[
            jax.ShapeDtypeStruct((n_rows, k_cols), BF16),
            jax.ShapeDtypeStruct((batch, t_rows, seq), BF16),
            jax.ShapeDtypeStruct((batch, n_heads, seq), F32),
        ],
        compiler_params=_compiler_params(("parallel",)),
        name="in_proj",
    )(x2d, g, w_k, w_t, wft)


def _forget_cumsum_kernel(ft_ref, b_ref, kf_ref, fend_ref):
    rows, seq = ft_ref.shape
    r = lax.broadcasted_iota(jnp.int32, (CUMSUM_CHUNK, CUMSUM_CHUNK), 0)
    c = lax.broadcasted_iota(jnp.int32, (CUMSUM_CHUNK, CUMSUM_CHUNK), 1)
    upper = (r <= c).astype(F32)
    pad = jnp.zeros((LANES - F_PIECES * F_PIECE_STRIDE, CUMSUM_CHUNK), F32)
    carry = jnp.zeros((rows, 1), F32)
    for ci in range(seq // CUMSUM_CHUNK):
        sl = slice(ci * CUMSUM_CHUNK, (ci + 1) * CUMSUM_CHUNK)
        log_f = _log_sigmoid(ft_ref[:, sl] + b_ref[...])
        local = jnp.dot(log_f, upper, precision=lax.Precision.HIGHEST,
                        preferred_element_type=F32)
        rest = (local + carry) * LOG2_E
        carry = carry + local[:, CUMSUM_CHUNK - 1:CUMSUM_CHUNK]
        chunk_end = (ci + 1) * CUMSUM_CHUNK
        if chunk_end % FOX_TILE == 0:
            blk = chunk_end // FOX_TILE - 1
            fend_ref[:, blk:blk + 1] = rest[:, CUMSUM_CHUNK - 1:CUMSUM_CHUNK]
        pieces = []
        for _ in range(F_PIECES):
            piece = rest.astype(BF16).astype(F32)
            pieces.append(piece)
            rest = rest - piece
        kf_ref[sl, :] = jnp.concatenate(pieces + [pad], axis=0).T.astype(BF16)


def _forget_cumsum(ft2d, bias_col):
    rows, seq = ft2d.shape
    assert rows == F_PIECE_STRIDE
    return pl.pallas_call(
        _forget_cumsum_kernel,
        grid=(1,),
        in_specs=[pl.BlockSpec((rows, seq), lambda i: (0, 0)),
                  pl.BlockSpec((rows, 1), lambda i: (0, 0))],
        out_specs=[pl.BlockSpec((seq, LANES), lambda i: (0, 0)),
                   pl.BlockSpec((rows, seq // FOX_TILE), lambda i: (0, 0))],
        out_shape=[jax.ShapeDtypeStruct((seq, LANES), BF16),
                   jax.ShapeDtypeStruct((rows, seq // FOX_TILE), F32)],
        compiler_params=_compiler_params(("arbitrary",)),
        name="forget_cumsum",
    )(ft2d, bias_col)


def _head_row_masks():
    sub = lax.broadcasted_iota(jnp.int32, (LANES, 1), 0)
    first = sub < HEAD_DIM
    return (first, jnp.logical_not(first))


def _key_block(ref, j, tile):
    k0 = pl.multiple_of(j * tile, tile)
    return ref[0, pl.ds(k0, tile), :]


def _value_block(ref, j, tile):
    k0 = pl.multiple_of(j * tile, tile)
    return ref[0, :, pl.ds(k0, tile)]


def _fox_kernel(qt_ref, k_ref, vt_ref, kf_ref, fend_ref, o_ref, s_scr, kmax_scr):
    b, p, i = pl.program_id(0), pl.program_id(1), pl.program_id(2)
    tile = qt_ref.shape[2]
    heads = range(s_scr.shape[1])
    row_masks = _head_row_masks()
    sub = lax.broadcasted_iota(jnp.int32, (LANES, 1), 0)
    f_rows_used = sub < F_PIECES * F_PIECE_STRIDE
    f_row = [(b * pl.num_programs(1) + p) * len(heads) + h for h in heads]

    def pair_part(h):
        pair = h // HEADS_PER_STEP
        return slice(pair * LANES, (pair + 1) * LANES)

    @pl.when(i == 0)
    def _():
        lane = lax.broadcasted_iota(jnp.int32, (1, k_ref.shape[2]), 1)
        col_max = jnp.max(jnp.abs(k_ref[0].astype(F32)), axis=0, keepdims=True)
        for h in heads:
            kmax_scr[h] = jnp.max(jnp.where(lane // HEAD_DIM == h, col_max, 0.0))

    qpt, qk_bound = [], []
    for h in heads:
        qt = qt_ref[0, pair_part(h), :]
        pick = jnp.logical_and(sub % F_PIECE_STRIDE == f_row[h], f_rows_used)
        f_sel = jnp.broadcast_to(jnp.where(pick, -1.0, 0.0).astype(BF16), qt.shape)
        q_head = jnp.where(row_masks[h % HEADS_PER_STEP], qt, jnp.zeros_like(qt))
        qpt.append(jnp.concatenate([q_head, f_sel], axis=0))
        q_l1 = jnp.sum(jnp.abs(q_head.astype(F32)), axis=0, keepdims=True)
        qk_bound.append(kmax_scr[h] * q_l1)
    key = lax.broadcasted_iota(jnp.int32, (tile, tile), 0)
    qry = lax.broadcasted_iota(jnp.int32, (tile, tile), 1)
    causal = key <= qry

    def issue(j, slot):
        k0 = pl.multiple_of(j * tile, tile)
        kj = _key_block(k_ref, j, tile)
        kfj = kf_ref[pl.ds(k0, tile), :]
        for h in heads:
            kp = jnp.concatenate([kj[:, pair_part(h)], kfj], axis=1)
            s_scr[slot, h] = jnp.dot(kp, qpt[h], preferred_element_type=F32)

    def consume(j, slot, state, masked):
        vt_all = _value_block(vt_ref, j, tile)
        out = []
        for h in heads:
            m, acc = state[h]
            vt = vt_all[pair_part(h)]

            def scores():
                s = s_scr[slot, h]
                return jnp.where(causal, s, -jnp.inf) if masked else s

            v_aug = jnp.where(row_masks[h % HEADS_PER_STEP], vt, jnp.ones_like(vt))
            m_new = jnp.maximum(m, jnp.max(scores(), axis=0, keepdims=True))
            alpha = jnp.exp2(m - m_new)
            prob = jnp.exp2(scores() - m_new).astype(BF16)
            acc = alpha * acc + jnp.dot(v_aug, prob, preferred_element_type=F32)
            out.append((m_new, acc))
        return tuple(out)

    def still_live(st, next_j):
        live = False
        for h in heads:
            m, _ = st[h]
            headroom = jnp.max(qk_bound[h] - m) - fend_ref[f_row[h], next_j]
            live = jnp.logical_or(live, headroom >= -DEAD_COST_LOG2)
        return live

    issue(i, 0)
    issue(jnp.maximum(i - 1, 0), 1)
    state = tuple((jnp.full((1, tile), -jnp.inf, F32), jnp.zeros((LANES, tile), F32))
                  for _ in heads)
    state = consume(i, 0, state, True)

    def sweep(first_j, n_trips, blocks_per_trip, ahead, state):
        def go(st, j):
            return still_live(st, jnp.maximum(j - ahead, 0))

        def trip(carry):
            u, st, _ = carry
            j = first_j - blocks_per_trip * u
            go_next = go(st, j - blocks_per_trip)
            for t in range(blocks_per_trip):
                issue(jnp.maximum(j - t - 1, 0), t % 2)
                st = consume(j - t, 1 - t % 2, st, False)
            return u + 1, st, go_next

        return lax.while_loop(lambda c: jnp.logical_and(c[0] < n_trips, c[2]), trip,
                              (jnp.int32(0), state, go(state, first_j)))

    long_trips, state, _ = sweep(i - 1, i // FOX_BLOCKS_PER_TRIP, FOX_BLOCKS_PER_TRIP,
                                 FOX_BLOCKS_PER_TRIP // 2, state)
    first_j = i - 1 - FOX_BLOCKS_PER_TRIP * long_trips
    pair_trips, state, live = sweep(first_j, (first_j + 1) // 2, 2, 0, state)
    left = first_j + 1 - 2 * pair_trips
    state = lax.cond(jnp.logical_and(left == 1, live),
                     lambda st: consume(0, 1, st, False), lambda st: st, state)
    o = [acc / pltpu.roll(acc, HEAD_DIM, 0) for _, acc in state]
    pair_out = [jnp.where(row_masks[0], o[h], o[h + 1]).T for h in heads[::HEADS_PER_STEP]]
    o_ref[0] = jnp.concatenate(pair_out, axis=1)


def _sb_kernel(qt_ref, k_ref, vt_ref, o_ref, z_scr):
    i = pl.program_id(2)
    tile = qt_ref.shape[2]
    heads = range(z_scr.shape[1])
    row_masks = _head_row_masks()

    def pair_part(h):
        pair = h // HEADS_PER_STEP
        return slice(pair * LANES, (pair + 1) * LANES)

    qmt = []
    for h in heads:
        qt = qt_ref[0, pair_part(h), :]
        qmt.append(jnp.where(row_masks[h % HEADS_PER_STEP], qt, jnp.zeros_like(qt)))
    key = lax.broadcasted_iota(jnp.int32, (tile, tile), 0)
    qry = lax.broadcasted_iota(jnp.int32, (tile, tile), 1)
    strict = key < qry
    tr = lax.broadcasted_iota(jnp.int32, (SUFFIX_BLOCK, SUFFIX_BLOCK), 0)
    tc = lax.broadcasted_iota(jnp.int32, (SUFFIX_BLOCK, SUFFIX_BLOCK), 1)
    not_before = (tc >= tr).astype(BF16)
    n_sub = tile // SUFFIX_BLOCK

    def issue(j, slot):
        kj = _key_block(k_ref, j, tile)
        for h in heads:
            z_scr[slot, h] = jnp.dot(kj[:, pair_part(h)], qmt[h],
                                     preferred_element_type=F32)

    def log_weights(slot, h, masked):
        z = z_scr[slot, h]
        cost = jnp.maximum(z, 0.0) + jnp.log2(1.0 + jnp.exp2(-jnp.abs(z)))
        if masked:
            cost = jnp.where(strict, cost, 0.0)
        below = jnp.zeros((1, tile), F32)
        parts = [None] * n_sub
        for u in reversed(range(n_sub)):
            x = cost[u * SUFFIX_BLOCK:(u + 1) * SUFFIX_BLOCK]
            hi = x.astype(BF16)
            lo = (x - hi.astype(F32)).astype(BF16)
            inclusive = (jnp.dot(not_before, hi, preferred_element_type=F32)
                         + jnp.dot(not_before, lo, preferred_element_type=F32))
            parts[u] = inclusive + below
            below = below + inclusive[0:1, :]
        return z_scr[slot, h] - jnp.concatenate(parts, axis=0), below

    def accumulate(j, logs, state, masked, valid=None):
        vt = _value_block(vt_ref, j, tile)
        out = []
        for h in heads:
            r_cost, acc = state[h]
            log_w, block_cost = logs[h]
            w = jnp.exp2(log_w)
            if masked:
                w = jnp.where(strict, w, 0.0)
            pv = jnp.exp2(-r_cost) * jnp.dot(vt[pair_part(h)], w.astype(BF16),
                                             preferred_element_type=F32)
            if valid is not None:
                pv = jnp.where(valid, pv, 0.0)
                block_cost = jnp.where(valid, block_cost, 0.0)
            out.append((r_cost + block_cost, acc + pv))
        return tuple(out)

    def consume(j, slot, state):
        return accumulate(j, [log_weights(slot, h, False) for h in heads], state, False)

    def still_live(st):
        lowest = functools.reduce(jnp.minimum, [jnp.min(r_cost) for r_cost, _ in st])
        return lowest < DEAD_COST_LOG2

    prev = jnp.maximum(i - 1, 0)
    issue(i, 0)
    issue(prev, 1)
    state = tuple((jnp.zeros((1, tile), F32), jnp.zeros((LANES, tile), F32))
                  for _ in heads)
    logs_diag = [log_weights(0, h, True) for h in heads]
    logs_prev = [log_weights(1, h, False) for h in heads]
    state = accumulate(i, logs_diag, state, True)
    state = accumulate(prev, logs_prev, state, False, valid=i >= 1)
    issue(jnp.maximum(i - 2, 0), 0)

    def block_pair(carry):
        u, st, _ = carry
        j = i - 2 - 2 * u
        issue(j - 1, 1)
        st = consume(j, 0, st)

        def second(st):
            issue(jnp.maximum(j - 2, 0), 0)
            return consume(j - 1, 1, st)

        st = lax.cond(still_live(st), second, lambda st: st, st)
        return u + 1, st, still_live(st)

    n_left = jnp.maximum(i - 1, 0)
    _, state, live = lax.while_loop(lambda c: jnp.logical_and(c[0] < n_left // 2, c[2]),
                                    block_pair, (jnp.int32(0), state, still_live(state)))
    state = lax.cond(jnp.logical_and(n_left % 2 == 1, live),
                     lambda st: consume(0, 0, st), lambda st: st, state)
    pair_out = [jnp.where(row_masks[0], state[h][1], state[h + 1][1]).T
                for h in heads[::HEADS_PER_STEP]]
    o_ref[0] = jnp.concatenate(pair_out, axis=1)


def _attention(kernel, k3, t3, vmem_extra, smem_extra, extra_scratch, *, tile, pairs_per_step,
               k_blk, q_blk, v_blk, out_width, name):
    batch, seq, _ = k3.shape
    width = LANES * pairs_per_step
    n_steps = out_width // width
    assert out_width % width == 0 and all(x % pairs_per_step == 0 for x in (k_blk, q_blk, v_blk))
    k_blk, q_blk, v_blk = (x // pairs_per_step for x in (k_blk, q_blk, v_blk))
    in_specs = [
        pl.BlockSpec((1, width, tile), lambda b, p, i: (b, q_blk + p, i)),
        pl.BlockSpec((1, seq, width), lambda b, p, i: (b, 0, k_blk + p)),
        pl.BlockSpec((1, width, seq), lambda b, p, i: (b, v_blk + p, 0)),
    ]
    in_specs += [pl.BlockSpec(a.shape, lambda b, p, i: (0, 0)) for a in vmem_extra]
    in_specs += [pl.BlockSpec(memory_space=pltpu.SMEM) for _ in smem_extra]
    return pl.pallas_call(
        kernel,
        grid=(batch, n_steps, seq // tile),
        in_specs=in_specs,
        out_specs=pl.BlockSpec((1, tile, width), lambda b, p, i: (b, i, p)),
        out_shape=jax.ShapeDtypeStruct((batch, seq, out_width), F32),
        scratch_shapes=[pltpu.VMEM((2, HEADS_PER_STEP * pairs_per_step, tile, tile), F32)]
        + extra_scratch,
        compiler_params=_compiler_params(("parallel", "parallel", "arbitrary")),
        name=name,
    )(t3, k3, t3, *vmem_extra, *smem_extra)


def _mix_ffn_kernel(of_ref, os_ref, x_ref, ofh_ref, osh_ref, xh_ref, gf_ref, gs_ref, wout_ref,
                    g_ref, wup_ref, cw_ref, cb_ref, wdn_ref, gfin_ref, o_ref, act_ref, *,
                    tiles_per_seq, d_ff):
    i = pl.program_id(0)
    wf = of_ref.shape[1]

    def attn_residual(o_fox, o_sb, x_in):
        a = _rmsnorm_bf16(o_fox, gf_ref[...])
        b = _rmsnorm_bf16(o_sb, gs_ref[...])
        return x_in + (jnp.dot(a, wout_ref[:wf, :], preferred_element_type=F32)
                       + jnp.dot(b, wout_ref[wf:, :], preferred_element_type=F32))

    x = attn_residual(of_ref[...], os_ref[...], x_ref[...])
    x_halo = attn_residual(ofh_ref[...], osh_ref[...], xh_ref[...])
    g = g_ref[...]
    h = _rmsnorm_bf16(x, g)
    h_halo = _rmsnorm_bf16(x_halo, g)
    h_halo = jnp.where(i % tiles_per_seq == 0, jnp.zeros_like(h_halo), h_halo)
    h_ext = jnp.concatenate([h_halo, h], axis=0)

    def conv(u_ext, cols):
        out = cb_ref[:, cols] + cw_ref[CONV_WIDTH - 1:CONV_WIDTH, cols] * u_ext[BF16_SUBLANES:]
        for back in range(1, CONV_WIDTH):
            shifted = pltpu.roll(u_ext, back, 0)[BF16_SUBLANES:]
            tap = CONV_WIDTH - 1 - back
            out = out + cw_ref[tap:tap + 1, cols] * shifted
        return out

    def chunk_cols(c):
        return (slice(c * FFN_CHUNK, (c + 1) * FFN_CHUNK),
                slice(d_ff + c * FFN_CHUNK, d_ff + (c + 1) * FFN_CHUNK))

    def up_proj(c):
        return tuple(jnp.dot(h_ext, wup_ref[:, cols], preferred_element_type=F32)
                     for cols in chunk_cols(c))

    n_chunks = d_ff // FFN_CHUNK
    u_next = up_proj(0)
    for c in range(n_chunks):
        gate_cols, val_cols = chunk_cols(c)
        u_gate, u_val = u_next
        if c + 1 < n_chunks:
            u_next = up_proj(c + 1)
        gate = conv(u_gate, gate_cols)
        val = conv(u_val, val_cols)
        act_ref[:, gate_cols] = (gate / (1.0 + jnp.exp(-gate)) * val).astype(BF16)

    x2 = x + jnp.dot(act_ref[...], wdn_ref[...], preferred_element_type=F32)
    ms = jnp.mean(x2 * x2, axis=-1, keepdims=True)
    o_ref[...] = x2 * lax.rsqrt(ms + EPS) * gfin_ref[...]


def _mix_ffn(o_fox, o_sb, x2d, g_fox, g_sb, w_out, g, w_up, conv_w, conv_b, w_down, g_final, *,
             seq):
    n_rows, d = x2d.shape
    wf, ws = o_fox.shape[1], o_sb.shape[1]
    d_ff = w_down.shape[0]
    tiles_per_seq = seq // ROW_TILE
    halo_blocks_per_tile = ROW_TILE // BF16_SUBLANES
    tile_rows = lambda i: (i, 0)
    halo_rows = lambda i: (jnp.maximum(i * halo_blocks_per_tile - 1, 0), 0)
    whole = lambda i: (0, 0)
    resident = dict(pipeline_mode=pl.Buffered(1))
    return pl.pallas_call(
        functools.partial(_mix_ffn_kernel, tiles_per_seq=tiles_per_seq, d_ff=d_ff),
        grid=(n_rows // ROW_TILE,),
        in_specs=[
            pl.BlockSpec((ROW_TILE, wf), tile_rows),
            pl.BlockSpec((ROW_TILE, ws), tile_rows),
            pl.BlockSpec((ROW_TILE, d), tile_rows),
            pl.BlockSpec((BF16_SUBLANES, wf), halo_rows),
            pl.BlockSpec((BF16_SUBLANES, ws), halo_rows),
            pl.BlockSpec((BF16_SUBLANES, d), halo_rows),
            pl.BlockSpec((1, wf), whole),
            pl.BlockSpec((1, ws), whole),
            pl.BlockSpec((wf + ws, d), whole, **resident),
            pl.BlockSpec((1, d), whole),
            pl.BlockSpec((d, 2 * d_ff), whole, **resident),
            pl.BlockSpec((CONV_WIDTH, 2 * d_ff), whole),
            pl.BlockSpec((1, 2 * d_ff), whole),
            pl.BlockSpec((d_ff, d), whole, **resident),
            pl.BlockSpec((1, d), whole),
        ],
        out_specs=pl.BlockSpec((ROW_TILE, d), tile_rows),
        out_shape=jax.ShapeDtypeStruct((n_rows, d), F32),
        scratch_shapes=[pltpu.VMEM((ROW_TILE, d_ff), BF16)],
        compiler_params=_compiler_params(("parallel",)),
        name="mix_ffn",
    )(o_fox, o_sb, x2d, o_fox, o_sb, x2d, g_fox, g_sb, w_out, g, w_up, conv_w, conv_b, w_down,
      g_final)


def kernel(x, attn_norm_g, w_in, forget_bias, fox_out_g, sb_out_g, w_out, ffn_norm_g, w_up,
           conv_w, conv_b, w_down, final_norm_g):
    batch, seq, d = x.shape
    depth = w_in.shape[0]
    n_fox = forget_bias.shape[1]
    fox_w = fox_out_g.shape[1]
    sb_w = sb_out_g.shape[1]
    assert seq % ROW_TILE == 0 and seq % FOX_TILE == 0 and seq % SB_TILE == 0
    assert SB_TILE % SUFFIX_BLOCK == 0
    assert fox_w == n_fox * HEAD_DIM and n_fox % HEADS_PER_STEP == 0
    assert fox_w % PROJ_CHUNK == 0 and sb_w % PROJ_CHUNK == 0
    assert w_down.shape[1] % FFN_CHUNK == 0

    scale = LOG2_E * HEAD_DIM ** -0.5
    fox_blocks = fox_w // LANES
    sb_blocks = sb_w // LANES
    c_fq, c_fk, c_fv, c_fl = 0, fox_w, 2 * fox_w, 3 * fox_w
    c_sq = c_fl + n_fox
    c_sk, c_sv = c_sq + sb_w, c_sq + 2 * sb_w

    x2d = x.reshape(batch * seq, d)
    for l in range(depth):
        w_l = w_in[l]
        cols = lambda start, width: w_l[:, start:start + width]
        w_k = jnp.concatenate([cols(c_fk, fox_w), cols(c_sk, sb_w)], axis=1).astype(BF16)
        w_t = jnp.concatenate([cols(c_fq, fox_w), cols(c_sq, sb_w),
                               cols(c_fv, fox_w), cols(c_sv, sb_w)], axis=1).T.astype(BF16)
        wft = cols(c_fl, n_fox).T.astype(BF16)
        k2d, t3, f_logit_t = _in_proj(x2d, attn_norm_g[l][None, :], w_k, w_t, wft,
                                      batch=batch, seq=seq, q_rows=fox_w + sb_w, scale=scale)
        bias_col = jnp.tile(forget_bias[l], batch)[:, None]
        kf, f_end = _forget_cumsum(f_logit_t.reshape(batch * n_fox, seq), bias_col)
        k3 = k2d.reshape(batch, seq, k2d.shape[1])

        o_fox = _attention(_fox_kernel, k3, t3, [kf], [f_end],
                           [pltpu.SMEM((HEADS_PER_STEP * FOX_PAIRS_PER_STEP,), F32)],
                           tile=FOX_TILE, pairs_per_step=FOX_PAIRS_PER_STEP, k_blk=0, q_blk=0,
                           v_blk=fox_blocks + sb_blocks, out_width=fox_w, name="fox_attn")
        o_sb = _attention(_sb_kernel, k3, t3, [], [], [], tile=SB_TILE,
                          pairs_per_step=SB_PAIRS_PER_STEP, k_blk=fox_blocks, q_blk=fox_blocks,
                          v_blk=2 * fox_blocks + sb_blocks, out_width=sb_w, name="sb_attn")

        assert l == depth - 1, "depth > 1 needs an un-normalised output between layers"
        x2d = _mix_ffn(o_fox.reshape(batch * seq, fox_w), o_sb.reshape(batch * seq, sb_w), x2d,
                       fox_out_g[l][None, :], sb_out_g[l][None, :], w_out[l].astype(BF16),
                       ffn_norm_g[l][None, :], w_up[l].astype(BF16), conv_w[l],
                       conv_b[l][None, :], w_down[l].astype(BF16), final_norm_g[None, :], seq=seq)
    return x2d.reshape(batch, seq, d)
```

```python
import functools

import jax
import jax.numpy as jnp
from jax import lax
from jax.experimental import pallas as pl
from jax.experimental.pallas import tpu as pltpu

HEAD_DIM = 64
EPS = 1e-6
CONV_WIDTH = 3

LANES = 128
HEADS_PER_STEP = LANES // HEAD_DIM
BF16_SUBLANES = 16
VMEM_LIMIT_BYTES = 56 * 1024 * 1024

ROW_TILE = 512
PROJ_CHUNK = 512
FOX_TILE = 512
SB_TILE = 256
SB_PAIRS_PER_STEP = 2
FOX_PAIRS_PER_STEP = 1
FOX_BLOCKS_PER_TRIP = 4
SUFFIX_BLOCK = 256
CUMSUM_CHUNK = 256
FFN_CHUNK = 256
F_PIECES = 3
F_PIECE_STRIDE = 32

F32 = jnp.float32
BF16 = jnp.bfloat16
_NT = (((1,), (1,)), ((), ()))
LOG2_E = 1.4426950408889634
DEAD_COST_LOG2 = 160.0


def _compiler_params(semantics):
    return pltpu.CompilerParams(dimension_semantics=semantics,
                                vmem_limit_bytes=VMEM_LIMIT_BYTES)


def _rmsnorm_bf16(x, g):
    ms = jnp.mean(x * x, axis=-1, keepdims=True)
    return (x * lax.rsqrt(ms + EPS) * g).astype(BF16)


def _log_sigmoid(x):
    return jnp.minimum(x, 0.0) - jnp.log1p(jnp.exp(-jnp.abs(x)))


def _in_proj_kernel(x_ref, g_ref, wk_ref, wt_ref, wft_ref, k_ref, t_ref, ft_ref, *,
                    q_rows, scale):
    h = _rmsnorm_bf16(x_ref[...], g_ref[...])
    for c in range(wk_ref.shape[1] // PROJ_CHUNK):
        sl = slice(c * PROJ_CHUNK, (c + 1) * PROJ_CHUNK)
        k_ref[:, sl] = jnp.dot(h, wk_ref[:, sl], preferred_element_type=F32).astype(BF16)
    for c in range(wt_ref.shape[0] // PROJ_CHUNK):
        sl = slice(c * PROJ_CHUNK, (c + 1) * PROJ_CHUNK)
        y = lax.dot_general(wt_ref[sl, :], h, _NT, preferred_element_type=F32)
        if (c + 1) * PROJ_CHUNK <= q_rows:
            y = y * scale
        t_ref[0, sl, :] = y.astype(BF16)
    ft_ref[0] = lax.dot_general(wft_ref[...], h, _NT, preferred_element_type=F32)


def _in_proj(x2d, g, w_k, w_t, wft, *, batch, seq, q_rows, scale):
    n_rows, d = x2d.shape
    k_cols = w_k.shape[1]
    t_rows = w_t.shape[0]
    n_heads = wft.shape[0]
    tiles_per_seq = seq // ROW_TILE
    pos_block = lambda i: (i // tiles_per_seq, 0, i % tiles_per_seq)
    return pl.pallas_call(
        functools.partial(_in_proj_kernel, q_rows=q_rows, scale=scale),
        grid=(n_rows // ROW_TILE,),
        in_specs=[
            pl.BlockSpec((ROW_TILE, d), lambda i: (i, 0)),
            pl.BlockSpec((1, d), lambda i: (0, 0)),
            pl.BlockSpec((d, k_cols), lambda i: (0, 0)),
            pl.BlockSpec((t_rows, d), lambda i: (0, 0)),
            pl.BlockSpec((n_heads, d), lambda i: (0, 0)),
        ],
        out_specs=[
            pl.BlockSpec((ROW_TILE, k_cols), lambda i: (i, 0)),
            pl.BlockSpec((1, t_rows, ROW_TILE), pos_block),
            pl.BlockSpec((1, n_heads, ROW_TILE), pos_block),
        ],
        out_shape=[
            jax.ShapeDtypeStruct((n_rows, k_cols), BF16),
            jax.ShapeDtypeStruct((batch, t_rows, seq), BF16),
            jax.ShapeDtypeStruct((batch, n_heads, seq), F32),
        ],
        compiler_params=_compiler_params(("parallel",)),
        name="in_proj",
    )(x2d, g, w_k, w_t, wft)


def _forget_cumsum_kernel(ft_ref, b_ref, kf_ref, fend_ref):
    rows, seq = ft_ref.shape
    r = lax.broadcasted_iota(jnp.int32, (CUMSUM_CHUNK, CUMSUM_CHUNK), 0)
    c = lax.broadcasted_iota(jnp.int32, (CUMSUM_CHUNK, CUMSUM_CHUNK), 1)
    upper = (r <= c).astype(F32)
    pad = jnp.zeros((LANES - F_PIECES * F_PIECE_STRIDE, CUMSUM_CHUNK), F32)
    carry = jnp.zeros((rows, 1), F32)
    for ci in range(seq // CUMSUM_CHUNK):
        sl = slice(ci * CUMSUM_CHUNK, (ci + 1) * CUMSUM_CHUNK)
        log_f = _log_sigmoid(ft_ref[:, sl] + b_ref[...])
        local = jnp.dot(log_f, upper, precision=lax.Precision.HIGHEST,
                        preferred_element_type=F32)
        rest = (local + carry) * LOG2_E
        carry = carry + local[:, CUMSUM_CHUNK - 1:CUMSUM_CHUNK]
        chunk_end = (ci + 1) * CUMSUM_CHUNK
        if chunk_end % FOX_TILE == 0:
            blk = chunk_end // FOX_TILE - 1
            fend_ref[:, blk:blk + 1] = rest[:, CUMSUM_CHUNK - 1:CUMSUM_CHUNK]
        pieces = []
        for _ in range(F_PIECES):
            piece = rest.astype(BF16).astype(F32)
            pieces.append(piece)
            rest = rest - piece
        kf_ref[sl, :] = jnp.concatenate(pieces + [pad], axis=0).T.astype(BF16)


def _forget_cumsum(ft2d, bias_col):
    rows, seq = ft2d.shape
    assert rows == F_PIECE_STRIDE
    return pl.pallas_call(
        _forget_cumsum_kernel,
        grid=(1,),
        in_specs=[pl.BlockSpec((rows, seq), lambda i: (0, 0)),
                  pl.BlockSpec((rows, 1), lambda i: (0, 0))],
        out_specs=[pl.BlockSpec((seq, LANES), lambda i: (0, 0)),
                   pl.BlockSpec((rows, seq // FOX_TILE), lambda i: (0, 0))],
        out_shape=[jax.ShapeDtypeStruct((seq, LANES), BF16),
                   jax.ShapeDtypeStruct((rows, seq // FOX_TILE), F32)],
        compiler_params=_compiler_params(("arbitrary",)),
        name="forget_cumsum",
    )(ft2d, bias_col)


def _head_row_masks():
    sub = lax.broadcasted_iota(jnp.int32, (LANES, 1), 0)
    first = sub < HEAD_DIM
    return (first, jnp.logical_not(first))


def _key_block(ref, j, tile):
    k0 = pl.multiple_of(j * tile, tile)
    return ref[0, pl.ds(k0, tile), :]


def _value_block(ref, j, tile):
    k0 = pl.multiple_of(j * tile, tile)
    return ref[0, :, pl.ds(k0, tile)]


def _fox_kernel(qt_ref, k_ref, vt_ref, kf_ref, fend_ref, o_ref, s_scr, kmax_scr):
    b, p, i = pl.program_id(0), pl.program_id(1), pl.program_id(2)
    tile = qt_ref.shape[2]
    heads = range(s_scr.shape[1])
    row_masks = _head_row_masks()
    sub = lax.broadcasted_iota(jnp.int32, (LANES, 1), 0)
    f_rows_used = sub < F_PIECES * F_PIECE_STRIDE
    f_row = [(b * pl.num_programs(1) + p) * len(heads) + h for h in heads]

    def pair_part(h):
        pair = h // HEADS_PER_STEP
        return slice(pair * LANES, (pair + 1) * LANES)

    @pl.when(i == 0)
    def _():
        lane = lax.broadcasted_iota(jnp.int32, (1, k_ref.shape[2]), 1)
        col_max = jnp.max(jnp.abs(k_ref[0].astype(F32)), axis=0, keepdims=True)
        for h in heads:
            kmax_scr[h] = jnp.max(jnp.where(lane // HEAD_DIM == h, col_max, 0.0))

    qpt, qk_bound = [], []
    for h in heads:
        qt = qt_ref[0, pair_part(h), :]
        pick = jnp.logical_and(sub % F_PIECE_STRIDE == f_row[h], f_rows_used)
        f_sel = jnp.broadcast_to(jnp.where(pick, -1.0, 0.0).astype(BF16), qt.shape)
        q_head = jnp.where(row_masks[h % HEADS_PER_STEP], qt, jnp.zeros_like(qt))
        qpt.append(jnp.concatenate([q_head, f_sel], axis=0))
        q_l1 = jnp.sum(jnp.abs(q_head.astype(F32)), axis=0, keepdims=True)
        qk_bound.append(kmax_scr[h] * q_l1)
    key = lax.broadcasted_iota(jnp.int32, (tile, tile), 0)
    qry = lax.broadcasted_iota(jnp.int32, (tile, tile), 1)
    causal = key <= qry

    def issue(j, slot):
        k0 = pl.multiple_of(j * tile, tile)
        kj = _key_block(k_ref, j, tile)
        kfj = kf_ref[pl.ds(k0, tile), :]
        for h in heads:
            kp = jnp.concatenate([kj[:, pair_part(h)], kfj], axis=1)
            s_scr[slot, h] = jnp.dot(kp, qpt[h], preferred_element_type=F32)

    def consume(j, slot, state, masked):
        vt_all = _value_block(vt_ref, j, tile)
        out = []
        for h in heads:
            m, acc = state[h]
            vt = vt_all[pair_part(h)]

            def scores():
                s = s_scr[slot, h]
                return jnp.where(causal, s, -jnp.inf) if masked else s

            v_aug = jnp.where(row_masks[h % HEADS_PER_STEP], vt, jnp.ones_like(vt))
            m_new = jnp.maximum(m, jnp.max(scores(), axis=0, keepdims=True))
            alpha = jnp.exp2(m - m_new)
            prob = jnp.exp2(scores() - m_new).astype(BF16)
            acc = alpha * acc + jnp.dot(v_aug, prob, preferred_element_type=F32)
            out.append((m_new, acc))
        return tuple(out)

    def still_live(st, next_j):
        live = False
        for h in heads:
            m, _ = st[h]
            headroom = jnp.max(qk_bound[h] - m) - fend_ref[f_row[h], next_j]
            live = jnp.logical_or(live, headroom >= -DEAD_COST_LOG2)
        return live

    issue(i, 0)
    issue(jnp.maximum(i - 1, 0), 1)
    state = tuple((jnp.full((1, tile), -jnp.inf, F32), jnp.zeros((LANES, tile), F32))
                  for _ in heads)
    state = consume(i, 0, state, True)

    def sweep(first_j, n_trips, blocks_per_trip, ahead, state):
        def go(st, j):
            return still_live(st, jnp.maximum(j - ahead, 0))

        def trip(carry):
            u, st, _ = carry
            j = first_j - blocks_per_trip * u
            go_next = go(st, j - blocks_per_trip)
            for t in range(blocks_per_trip):
                issue(jnp.maximum(j - t - 1, 0), t % 2)
                st = consume(j - t, 1 - t % 2, st, False)
            return u + 1, st, go_next

        return lax.while_loop(lambda c: jnp.logical_and(c[0] < n_trips, c[2]), trip,
                              (jnp.int32(0), state, go(state, first_j)))

    long_trips, state, _ = sweep(i - 1, i // FOX_BLOCKS_PER_TRIP, FOX_BLOCKS_PER_TRIP,
                                 FOX_BLOCKS_PER_TRIP // 2, state)
    first_j = i - 1 - FOX_BLOCKS_PER_TRIP * long_trips
    pair_trips, state, live = sweep(first_j, (first_j + 1) // 2, 2, 0, state)
    left = first_j + 1 - 2 * pair_trips
    state = lax.cond(jnp.logical_and(left == 1, live),
                     lambda st: consume(0, 1, st, False), lambda st: st, state)
    o = [acc / pltpu.roll(acc, HEAD_DIM, 0) for _, acc in state]
    pair_out = [jnp.where(row_masks[0], o[h], o[h + 1]).T for h in heads[::HEADS_PER_STEP]]
    o_ref[0] = jnp.concatenate(pair_out, axis=1)


def _sb_kernel(qt_ref, k_ref, vt_ref, o_ref, z_scr):
    i = pl.program_id(2)
    tile = qt_ref.shape[2]
    heads = range(z_scr.shape[1])
    row_masks = _head_row_masks()

    def pair_part(h):
        pair = h // HEADS_PER_STEP
        return slice(pair * LANES, (pair + 1) * LANES)

    qmt = []
    for h in heads:
        qt = qt_ref[0, pair_part(h), :]
        qmt.append(jnp.where(row_masks[h % HEADS_PER_STEP], qt, jnp.zeros_like(qt)))
    key = lax.broadcasted_iota(jnp.int32, (tile, tile), 0)
    qry = lax.broadcasted_iota(jnp.int32, (tile, tile), 1)
    strict = key < qry
    tr = lax.broadcasted_iota(jnp.int32, (SUFFIX_BLOCK, SUFFIX_BLOCK), 0)
    tc = lax.broadcasted_iota(jnp.int32, (SUFFIX_BLOCK, SUFFIX_BLOCK), 1)
    after = (tc > tr).astype(BF16)
    n_sub = tile // SUFFIX_BLOCK

    def issue(j, slot):
        kj = _key_block(k_ref, j, tile)
        for h in heads:
            z_scr[slot, h] = jnp.dot(kj[:, pair_part(h)], qmt[h],
                                     preferred_element_type=F32)

    def log_weights(slot, h, masked):
        z = z_scr[slot, h]
        cost = jnp.maximum(z, 0.0) + jnp.log2(1.0 + jnp.exp2(-jnp.abs(z)))
        if masked:
            cost = jnp.where(strict, cost, 0.0)
        below = jnp.zeros((1, tile), F32)
        parts = [None] * n_sub
        for u in reversed(range(n_sub)):
            x = cost[u * SUFFIX_BLOCK:(u + 1) * SUFFIX_BLOCK]
            later = jnp.dot(after, x.astype(BF16), preferred_element_type=F32)
            parts[u] = later + below
            below = below + later[0:1, :] + x[0:1, :]
        return (z_scr[slot, h] - cost) - jnp.concatenate(parts, axis=0), below

    def accumulate(j, logs, state, masked, valid=None):
        vt = _value_block(vt_ref, j, tile)
        out = []
        for h in heads:
            r_cost, acc = state[h]
            log_w, block_cost = logs[h]
            w = jnp.exp2(log_w)
            if masked:
                w = jnp.where(strict, w, 0.0)
            pv = jnp.exp2(-r_cost) * jnp.dot(vt[pair_part(h)], w.astype(BF16),
                                             preferred_element_type=F32)
            if valid is not None:
                pv = jnp.where(valid, pv, 0.0)
                block_cost = jnp.where(valid, block_cost, 0.0)
            out.append((r_cost + block_cost, acc + pv))
        return tuple(out)

    def consume(j, slot, state):
        return accumulate(j, [log_weights(slot, h, False) for h in heads], state, False)

    def still_live(st):
        lowest = functools.reduce(jnp.minimum, [jnp.min(r_cost) for r_cost, _ in st])
        return lowest < DEAD_COST_LOG2

    prev = jnp.maximum(i - 1, 0)
    issue(i, 0)
    issue(prev, 1)
    state = tuple((jnp.zeros((1, tile), F32), jnp.zeros((LANES, tile), F32))
                  for _ in heads)
    logs_diag = [log_weights(0, h, True) for h in heads]
    logs_prev = [log_weights(1, h, False) for h in heads]
    state = accumulate(i, logs_diag, state, True)
    state = accumulate(prev, logs_prev, state, False, valid=i >= 1)
    issue(jnp.maximum(i - 2, 0), 0)

    def block_pair(carry):
        u, st, _ = carry
        j = i - 2 - 2 * u
        issue(j - 1, 1)
        st = consume(j, 0, st)

        def second(st):
            issue(jnp.maximum(j - 2, 0), 0)
            return consume(j - 1, 1, st)

        st = lax.cond(still_live(st), second, lambda st: st, st)
        return u + 1, st, still_live(st)

    n_left = jnp.maximum(i - 1, 0)
    _, state, live = lax.while_loop(lambda c: jnp.logical_and(c[0] < n_left // 2, c[2]),
                                    block_pair, (jnp.int32(0), state, still_live(state)))
    state = lax.cond(jnp.logical_and(n_left % 2 == 1, live),
                     lambda st: consume(0, 0, st), lambda st: st, state)
    pair_out = [jnp.where(row_masks[0], state[h][1], state[h + 1][1]).T
                for h in heads[::HEADS_PER_STEP]]
    o_ref[0] = jnp.concatenate(pair_out, axis=1)


def _attention(kernel, k3, t3, vmem_extra, smem_extra, extra_scratch, *, tile, pairs_per_step,
               k_blk, q_blk, v_blk, out_width, name):
    batch, seq, _ = k3.shape
    width = LANES * pairs_per_step
    n_steps = out_width // width
    assert out_width % width == 0 and all(x % pairs_per_step == 0 for x in (k_blk, q_blk, v_blk))
    k_blk, q_blk, v_blk = (x // pairs_per_step for x in (k_blk, q_blk, v_blk))
    in_specs = [
        pl.BlockSpec((1, width, tile), lambda b, p, i: (b, q_blk + p, i)),
        pl.BlockSpec((1, seq, width), lambda b, p, i: (b, 0, k_blk + p)),
        pl.BlockSpec((1, width, seq), lambda b, p, i: (b, v_blk + p, 0)),
    ]
    in_specs += [pl.BlockSpec(a.shape, lambda b, p, i: (0, 0)) for a in vmem_extra]
    in_specs += [pl.BlockSpec(memory_space=pltpu.SMEM) for _ in smem_extra]
    return pl.pallas_call(
        kernel,
        grid=(batch, n_steps, seq // tile),
        in_specs=in_specs,
        out_specs=pl.BlockSpec((1, tile, width), lambda b, p, i: (b, i, p)),
        out_shape=jax.ShapeDtypeStruct((batch, seq, out_width), F32),
        scratch_shapes=[pltpu.VMEM((2, HEADS_PER_STEP * pairs_per_step, tile, tile), F32)]
        + extra_scratch,
        compiler_params=_compiler_params(("parallel", "parallel", "arbitrary")),
        name=name,
    )(t3, k3, t3, *vmem_extra, *smem_extra)


def _mix_ffn_kernel(of_ref, os_ref, x_ref, ofh_ref, osh_ref, xh_ref, gf_ref, gs_ref, wout_ref,
                    g_ref, wup_ref, cw_ref, cb_ref, wdn_ref, gfin_ref, o_ref, act_ref, *,
                    tiles_per_seq, d_ff):
    i = pl.program_id(0)
    wf = of_ref.shape[1]

    def attn_residual(o_fox, o_sb, x_in):
        a = _rmsnorm_bf16(o_fox, gf_ref[...])
        b = _rmsnorm_bf16(o_sb, gs_ref[...])
        return x_in + (jnp.dot(a, wout_ref[:wf, :], preferred_element_type=F32)
                       + jnp.dot(b, wout_ref[wf:, :], preferred_element_type=F32))

    x = attn_residual(of_ref[...], os_ref[...], x_ref[...])
    x_halo = attn_residual(ofh_ref[...], osh_ref[...], xh_ref[...])
    g = g_ref[...]
    h = _rmsnorm_bf16(x, g)
    h_halo = _rmsnorm_bf16(x_halo, g)
    h_halo = jnp.where(i % tiles_per_seq == 0, jnp.zeros_like(h_halo), h_halo)
    h_ext = jnp.concatenate([h_halo, h], axis=0)

    def conv(u_ext, cols):
        out = cb_ref[:, cols] + cw_ref[CONV_WIDTH - 1:CONV_WIDTH, cols] * u_ext[BF16_SUBLANES:]
        for back in range(1, CONV_WIDTH):
            shifted = pltpu.roll(u_ext, back, 0)[BF16_SUBLANES:]
            tap = CONV_WIDTH - 1 - back
            out = out + cw_ref[tap:tap + 1, cols] * shifted
        return out

    def chunk_cols(c):
        return (slice(c * FFN_CHUNK, (c + 1) * FFN_CHUNK),
                slice(d_ff + c * FFN_CHUNK, d_ff + (c + 1) * FFN_CHUNK))

    def up_proj(c):
        return tuple(jnp.dot(h_ext, wup_ref[:, cols], preferred_element_type=F32)
                     for cols in chunk_cols(c))

    n_chunks = d_ff // FFN_CHUNK
    u_next = up_proj(0)
    for c in range(n_chunks):
        gate_cols, val_cols = chunk_cols(c)
        u_gate, u_val = u_next
        if c + 1 < n_chunks:
            u_next = up_proj(c + 1)
        gate = conv(u_gate, gate_cols)
        val = conv(u_val, val_cols)
        act_ref[:, gate_cols] = (gate / (1.0 + jnp.exp(-gate)) * val).astype(BF16)

    x2 = x + jnp.dot(act_ref[...], wdn_ref[...], preferred_element_type=F32)
    ms = jnp.mean(x2 * x2, axis=-1, keepdims=True)
    o_ref[...] = x2 * lax.rsqrt(ms + EPS) * gfin_ref[...]


def _mix_ffn(o_fox, o_sb, x2d, g_fox, g_sb, w_out, g, w_up, conv_w, conv_b, w_down, g_final, *,
             seq):
    n_rows, d = x2d.shape
    wf, ws = o_fox.shape[1], o_sb.shape[1]
    d_ff = w_down.shape[0]
    tiles_per_seq = seq // ROW_TILE
    halo_blocks_per_tile = ROW_TILE // BF16_SUBLANES
    tile_rows = lambda i: (i, 0)
    halo_rows = lambda i: (jnp.maximum(i * halo_blocks_per_tile - 1, 0), 0)
    whole = lambda i: (0, 0)
    resident = dict(pipeline_mode=pl.Buffered(1))
    return pl.pallas_call(
        functools.partial(_mix_ffn_kernel, tiles_per_seq=tiles_per_seq, d_ff=d_ff),
        grid=(n_rows // ROW_TILE,),
        in_specs=[
            pl.BlockSpec((ROW_TILE, wf), tile_rows),
            pl.BlockSpec((ROW_TILE, ws), tile_rows),
            pl.BlockSpec((ROW_TILE, d), tile_rows),
            pl.BlockSpec((BF16_SUBLANES, wf), halo_rows),
            pl.BlockSpec((BF16_SUBLANES, ws), halo_rows),
            pl.BlockSpec((BF16_SUBLANES, d), halo_rows),
            pl.BlockSpec((1, wf), whole),
            pl.BlockSpec((1, ws), whole),
            pl.BlockSpec((wf + ws, d), whole, **resident),
            pl.BlockSpec((1, d), whole),
            pl.BlockSpec((d, 2 * d_ff), whole, **resident),
            pl.BlockSpec((CONV_WIDTH, 2 * d_ff), whole),
            pl.BlockSpec((1, 2 * d_ff), whole),
            pl.BlockSpec((d_ff, d), whole, **resident),
            pl.BlockSpec((1, d), whole),
        ],
        out_specs=pl.BlockSpec((ROW_TILE, d), tile_rows),
        out_shape=jax.ShapeDtypeStruct((n_rows, d), F32),
        scratch_shapes=[pltpu.VMEM((ROW_TILE, d_ff), BF16)],
        compiler_params=_compiler_params(("parallel",)),
        name="mix_ffn",
    )(o_fox, o_sb, x2d, o_fox, o_sb, x2d, g_fox, g_sb, w_out, g, w_up, conv_w, conv_b, w_down,
      g_final)


def kernel(x, attn_norm_g, w_in, forget_bias, fox_out_g, sb_out_g, w_out, ffn_norm_g, w_up,
           conv_w, conv_b, w_down, final_norm_g):
    batch, seq, d = x.shape
    depth = w_in.shape[0]
    n_fox = forget_bias.shape[1]
    fox_w = fox_out_g.shape[1]
    sb_w = sb_out_g.shape[1]
    assert seq % ROW_TILE == 0 and seq % FOX_TILE == 0 and seq % SB_TILE == 0
    assert SB_TILE % SUFFIX_BLOCK == 0
    assert fox_w == n_fox * HEAD_DIM and n_fox % HEADS_PER_STEP == 0
    assert fox_w % PROJ_CHUNK == 0 and sb_w % PROJ_CHUNK == 0
    assert w_down.shape[1] % FFN_CHUNK == 0

    scale = LOG2_E * HEAD_DIM ** -0.5
    fox_blocks = fox_w // LANES
    sb_blocks = sb_w // LANES
    c_fq, c_fk, c_fv, c_fl = 0, fox_w, 2 * fox_w, 3 * fox_w
    c_sq = c_fl + n_fox
    c_sk, c_sv = c_sq + sb_w, c_sq + 2 * sb_w

    x2d = x.reshape(batch * seq, d)
    for l in range(depth):
        w_l = w_in[l]
        cols = lambda start, width: w_l[:, start:start + width]
        w_k = jnp.concatenate([cols(c_fk, fox_w), cols(c_sk, sb_w)], axis=1).astype(BF16)
        w_t = jnp.concatenate([cols(c_fq, fox_w), cols(c_sq, sb_w),
                               cols(c_fv, fox_w), cols(c_sv, sb_w)], axis=1).T.astype(BF16)
        wft = cols(c_fl, n_fox).T.astype(BF16)
        k2d, t3, f_logit_t = _in_proj(x2d, attn_norm_g[l][None, :], w_k, w_t, wft,
                                      batch=batch, seq=seq, q_rows=fox_w + sb_w, scale=scale)
        bias_col = jnp.tile(forget_bias[l], batch)[:, None]
        kf, f_end = _forget_cumsum(f_logit_t.reshape(batch * n_fox, seq), bias_col)
        k3 = k2d.reshape(batch, seq, k2d.shape[1])

        o_fox = _attention(_fox_kernel, k3, t3, [kf], [f_end],
                           [pltpu.SMEM((HEADS_PER_STEP * FOX_PAIRS_PER_STEP,), F32)],
                           tile=FOX_TILE, pairs_per_step=FOX_PAIRS_PER_STEP, k_blk=0, q_blk=0,
                           v_blk=fox_blocks + sb_blocks, out_width=fox_w, name="fox_attn")
        o_sb = _attention(_sb_kernel, k3, t3, [], [], [], tile=SB_TILE,
                          pairs_per_step=SB_PAIRS_PER_STEP, k_blk=fox_blocks, q_blk=fox_blocks,
                          v_blk=2 * fox_blocks + sb_blocks, out_width=sb_w, name="sb_attn")

        assert l == depth - 1, "depth > 1 needs an un-normalised output between layers"
        x2d = _mix_ffn(o_fox.reshape(batch * seq, fox_w), o_sb.reshape(batch * seq, sb_w), x2d,
                       fox_out_g[l][None, :], sb_out_g[l][None, :], w_out[l].astype(BF16),
                       ffn_norm_g[l][None, :], w_up[l].astype(BF16), conv_w[l],
                       conv_b[l][None, :], w_down[l].astype(BF16), final_norm_g[None, :], seq=seq)
    return x2d.reshape(batch, seq, d)
```
